```python
import math
import jax, jax.numpy as jnp
from jax import lax
import numpy as np

D_MODEL = 1024
BATCH = 4
SEQ = 4096
DEPTH = 2

CHUNK = 64
EXPAND = 2
D_MIX = EXPAND * D_MODEL
N_GROUPS = 4
G = D_MIX // N_GROUPS
HEAD_DIM = 64
N_HEADS = G // HEAD_DIM
CONV_W = 4
LORA_W = 64
LORA_A = 64
RWKV_SHIFT = 3 * G + LORA_W + LORA_A
LRU_C = 8.0
W_DECAY_SCALE = 0.606531
ROPE_THETA = 10000.0
NORM_EPS = 1e-6
GN_EPS = 1e-5
D_IN = (2 * G + 3 * G + 2 * N_HEADS) + (RWKV_SHIFT + G) + (2 * G) + (4 * G)

kernel_name = "hybrid_parallel_mlstm_rwkv7_rglru_retention"


def rms_norm(x, g):
    xf = x.astype(jnp.float32)
    y = xf * lax.rsqrt(jnp.mean(xf * xf, -1, keepdims=True) + NORM_EPS)
    return (y * g.astype(jnp.float32)).astype(x.dtype)


def heads(t):
    return t.reshape(*t.shape[:-1], N_HEADS, HEAD_DIM)


def head_layer_norm(x, scale):
    xf = x.astype(jnp.float32)
    mu = jnp.mean(xf, -1, keepdims=True)
    var = jnp.mean(jnp.square(xf - mu), -1, keepdims=True)
    y = ((xf - mu) * lax.rsqrt(var + GN_EPS)).reshape(*x.shape[:-2], -1)
    return y * scale.astype(jnp.float32)


def causal_dwconv(x, w, b):
    c = x.shape[-1]
    y = lax.conv_general_dilated(
        x, w[:, None, :].astype(x.dtype), window_strides=(1,), padding=[(CONV_W - 1, 0)],
        dimension_numbers=("NWC", "WIO", "NWC"), feature_group_count=c)
    return y + b.astype(x.dtype)


def token_shift(x):
    return jnp.pad(x, ((0, 0), (1, 0), (0, 0)))[:, :-1]


def rope(x, cos, sin):
    half = HEAD_DIM // 2
    x1, x2 = x[..., :half], x[..., half:]
    return jnp.concatenate([x1 * cos - x2 * sin, x2 * cos + x1 * sin], -1)


def to_chunks(t):
    b, s, h, d = t.shape
    return t.reshape(b, s // CHUNK, CHUNK, h, d).transpose(1, 0, 3, 2, 4)


def from_chunks(t):
    nc, b, h, l, d = t.shape
    return t.transpose(1, 0, 3, 2, 4).reshape(b, nc * l, h, d)


def gate_chunks(t):
    b, s, h = t.shape
    return t.reshape(b, s // CHUNK, CHUNK, h).transpose(1, 0, 3, 2)


def mlstm_chunkwise(q, k, v, i_pre, f_pre):
    b, s, h, dh = q.shape
    f32 = jnp.float32
    qc = to_chunks(q.astype(f32))
    kc = to_chunks(k.astype(f32)) * (dh ** -0.5)
    vc = to_chunks(v.astype(f32))
    ic = gate_chunks(i_pre.astype(f32))
    lfc = gate_chunks(jax.nn.log_sigmoid(f_pre.astype(f32)))
    causal = jnp.tril(jnp.ones((CHUNK, CHUNK), bool))

    def step(carry, inp):
        c_st, n_st, m_st = carry
        qb, kb, vb, ib, lfb = inp
        cum = jnp.cumsum(lfb, -1)
        log_d = jnp.where(causal, cum[..., :, None] - cum[..., None, :] + ib[..., None, :], -jnp.inf)
        inter = cum + m_st[..., None]
        m_t = jnp.maximum(inter, jnp.max(log_d, -1))
        scores = jnp.einsum("bhld,bhsd->bhls", qb, kb) * jnp.exp(log_d - m_t[..., None])
        s_inter = jnp.exp(inter - m_t)
        num = jnp.einsum("bhls,bhsd->bhld", scores, vb) + s_inter[..., None] * jnp.einsum("bhld,bhed->bhle", qb, c_st)
        den = jnp.sum(scores, -1) + s_inter * jnp.einsum("bhld,bhd->bhl", qb, n_st)
        h_out = num / jnp.maximum(jnp.abs(den), jnp.exp(-m_t))[..., None]
        tot = cum[..., -1]
        log_w = tot[..., None] - cum + ib
        m_new = jnp.maximum(tot + m_st, jnp.max(log_w, -1))
        wj = jnp.exp(log_w - m_new[..., None])
        sc = jnp.exp(tot + m_st - m_new)
        c_new = sc[..., None, None] * c_st + jnp.einsum("bhl,bhle,bhld->bhed", wj, vb, kb)
        n_new = sc[..., None] * n_st + jnp.einsum("bhl,bhld->bhd", wj, kb)
        return (c_new, n_new, m_new), h_out

    init = (jnp.zeros((b, h, dh, dh), f32), jnp.zeros((b, h, dh), f32), jnp.zeros((b, h), f32))
    _, hc = lax.scan(step, init, (qc, kc, vc, ic, lfc))
    return from_chunks(hc)


def rwkv7_scan(r, w, k_t, v, kappa_hat, a):
    b, s, h, dh = r.shape

    def step(st, inp):
        r_t, w_t, k_tt, v_t, kh_t, a_t = inp
        sk = jnp.einsum("bhvk,bhk->bhv", st, kh_t)
        st = st * w_t[:, :, None, :] - sk[..., None] * (a_t * kh_t)[:, :, None, :] + v_t[..., None] * k_tt[:, :, None, :]
        return st, jnp.einsum("bhvk,bhk->bhv", st, r_t)

    xs = tuple(jnp.moveaxis(t, 1, 0) for t in (r, w, k_t, v, kappa_hat, a))
    _, y = lax.scan(step, jnp.zeros((b, h, dh, dh), jnp.float32), xs)
    return jnp.moveaxis(y, 0, 1)


def rglru(x, w_r, b_r, w_i, b_i, lam):
    xf = x.astype(jnp.float32)
    xh = heads(xf)
    r = jax.nn.sigmoid(jnp.einsum("bshi,hij->bshj", xh, w_r.astype(jnp.float32)).reshape(xf.shape) + b_r)
    i = jax.nn.sigmoid(jnp.einsum("bshi,hij->bshj", xh, w_i.astype(jnp.float32)).reshape(xf.shape) + b_i)
    log_a = -LRU_C * r * jax.nn.softplus(-lam.astype(jnp.float32))
    a = jnp.exp(log_a)
    u = jnp.sqrt(-jnp.expm1(2.0 * log_a)) * (i * xf)

    def combine(left, right):
        a1, b1 = left
        a2, b2 = right
        return a1 * a2, a2 * b1 + b2

    _, h = lax.associative_scan(combine, (a, u), axis=1)
    return h


def retention_chunkwise(q, k, v):
    b, s, h, dh = q.shape
    f32 = jnp.float32
    log_g = jnp.log1p(-jnp.exp2(-5.0 - jnp.arange(N_HEADS, dtype=f32)))
    idx = jnp.arange(CHUNK, dtype=f32)
    d_mat = jnp.exp(log_g[:, None, None] * jnp.abs(idx[:, None] - idx[None, :]))
    xi = jnp.exp(log_g[:, None] * (idx + 1.0))
    zeta = jnp.exp(log_g[:, None] * (CHUNK - 1.0 - idx))
    g_chunk = jnp.exp(log_g * CHUNK)

    def step(r_st, inp):
        qb, kb, vb = inp
        intra = jnp.einsum("bhld,bhsd->bhls", qb, kb) * d_mat
        o = jnp.einsum("bhls,bhse->bhle", intra, vb) + xi[..., None] * jnp.einsum("bhld,bhde->bhle", qb, r_st)
        r_st = g_chunk[:, None, None] * r_st + jnp.einsum("bhsd,bhse->bhde", kb * zeta[..., None], vb)
        return r_st, o

    xs = (to_chunks(q.astype(f32)), to_chunks(k.astype(f32)), to_chunks(v.astype(f32)))
    _, oc = lax.scan(step, jnp.zeros((b, h, dh, dh), f32), xs)
    return from_chunks(oc)


def split_at(t, sizes):
    idx = [int(i) for i in np.cumsum(sizes)[:-1]]
    return jnp.split(t, idx, axis=-1)


def setup_inputs(seed: int = 0) -> dict:
    key = jax.random.key(seed)
    ks = jax.random.split(key, 26)
    n = lambda k, shape: jax.random.normal(k, shape, jnp.float32)
    u_lam = jax.random.uniform(ks[24], (DEPTH, G), jnp.float32, 0.9, 0.999)
    s_lam = u_lam ** (1.0 / LRU_C)
    f_bias = jnp.linspace(3.0, 6.0, N_HEADS, dtype=jnp.float32)[None, :] + 0.1 * n(ks[8], (DEPTH, N_HEADS))
    return {
        "x": n(ks[0], (BATCH, SEQ, D_MODEL)),
        "norm_pre": 1.0 + 0.05 * n(ks[1], (DEPTH, D_MODEL)),
        "norm_post": 1.0 + 0.05 * n(ks[2], (DEPTH, D_MODEL)),
        "w_in": n(ks[3], (DEPTH, D_MODEL, D_IN)) * D_MODEL ** -0.5,
        "w_out": n(ks[4], (DEPTH, D_MIX, D_MODEL)) * D_MIX ** -0.5,
        "mlstm_conv_w": n(ks[5], (DEPTH, CONV_W, 2 * G)) * CONV_W ** -0.5,
        "mlstm_conv_b": 0.01 * n(ks[6], (DEPTH, 2 * G)),
        "mlstm_i_bias": 0.1 * n(ks[7], (DEPTH, N_HEADS)),
        "mlstm_f_bias": f_bias,
        "mlstm_norm_w": 1.0 + 0.05 * n(ks[9], (DEPTH, G)),
        "rwkv_mu": jax.random.uniform(ks[10], (DEPTH, RWKV_SHIFT), jnp.float32),
        "rwkv_w_up": 0.1 * n(ks[11], (DEPTH, LORA_W, G)),
        "rwkv_w0": jax.random.uniform(ks[12], (DEPTH, G), jnp.float32, -4.0, 2.0),
        "rwkv_a_up": 0.1 * n(ks[13], (DEPTH, LORA_A, G)),
        "rwkv_a0": 0.1 * n(ks[14], (DEPTH, G)),
        "rwkv_k_k": 0.85 + 0.05 * n(ks[15], (DEPTH, G)),
        "rwkv_k_a": 1.0 + 0.05 * n(ks[16], (DEPTH, G)),
        "rwkv_r_k": 0.1 * n(ks[17], (DEPTH, G)),
        "rwkv_gn_w": 1.0 + 0.05 * n(ks[18], (DEPTH, G)),
        "rwkv_gn_b": 0.01 * n(ks[19], (DEPTH, G)),
        "lru_conv_w": n(ks[20], (DEPTH, CONV_W, G)) * CONV_W ** -0.5,
        "lru_conv_b": 0.01 * n(ks[21], (DEPTH, G)),
        "lru_w_r": n(ks[22], (DEPTH, N_HEADS, HEAD_DIM, HEAD_DIM)) * HEAD_DIM ** -0.5,
        "lru_b_r": 0.01 * n(ks[23], (DEPTH, G)),
        "lru_w_i": n(ks[25], (DEPTH, N_HEADS, HEAD_DIM, HEAD_DIM)) * HEAD_DIM ** -0.5,
        "lru_b_i": 0.01 * n(jax.random.fold_in(ks[23], 1), (DEPTH, G)),
        "lru_lambda": jnp.log(s_lam) - jnp.log1p(-s_lam),
        "ret_norm_w": 1.0 + 0.05 * n(jax.random.fold_in(ks[24], 1), (DEPTH, G)),
    }


def reference(x, norm_pre, norm_post, w_in, w_out, mlstm_conv_w, mlstm_conv_b, mlstm_i_bias,
              mlstm_f_bias, mlstm_norm_w, rwkv_mu, rwkv_w_up, rwkv_w0, rwkv_a_up, rwkv_a0,
              rwkv_k_k, rwkv_k_a, rwkv_r_k, rwkv_gn_w, rwkv_gn_b, lru_conv_w, lru_conv_b,
              lru_w_r, lru_b_r, lru_w_i, lru_b_i, lru_lambda, ret_norm_w):
    f32 = jnp.float32
    seq = x.shape[1]
    half = HEAD_DIM // 2
    pos = jnp.arange(seq, dtype=f32)
    inv_freq = ROPE_THETA ** (-jnp.arange(half, dtype=f32) / half)
    ang = pos[:, None] * inv_freq[None, :]
    cos = jnp.cos(ang)[:, None, :]
    sin = jnp.sin(ang)[:, None, :]
    silu = lambda t: jax.nn.silu(t.astype(f32))

    for l in range(DEPTH):
        h = rms_norm(x, norm_pre[l])
        p = h @ w_in[l]
        (a_qk, a_v, a_o, a_z, a_i, a_f, b_sh, b_z, c_x, c_z, d_q, d_k, d_v, d_z) = split_at(
            p, [2 * G, G, G, G, N_HEADS, N_HEADS, RWKV_SHIFT, G, G, G, G, G, G, G])

        qk = jax.nn.silu(causal_dwconv(a_qk, mlstm_conv_w[l], mlstm_conv_b[l]))
        q_a, k_a = jnp.split(qk, 2, axis=-1)
        h_a = mlstm_chunkwise(heads(q_a), heads(k_a), heads(a_v),
                              a_i + mlstm_i_bias[l], a_f + mlstm_f_bias[l])
        h_a = h_a * jax.nn.sigmoid(heads(a_o).astype(f32))
        y_a = head_layer_norm(h_a, mlstm_norm_w[l]) * silu(a_z)

        b_sh = b_sh + rwkv_mu[l] * (token_shift(b_sh) - b_sh)
        r_b, k_b, v_b, w_lo, a_lo = split_at(b_sh.astype(f32), [G, G, G, LORA_W, LORA_A])
        decay_logit = rwkv_w0[l] + jnp.tanh(w_lo) @ rwkv_w_up[l].astype(f32)
        w_b = jnp.exp(-W_DECAY_SCALE * jax.nn.sigmoid(decay_logit))
        a_b = jax.nn.sigmoid(rwkv_a0[l] + a_lo @ rwkv_a_up[l].astype(f32))
        kappa = heads(k_b * rwkv_k_k[l])
        kappa_hat = kappa * lax.rsqrt(jnp.sum(kappa * kappa, -1, keepdims=True) + 1e-12)
        k_til = k_b * (1.0 + (a_b - 1.0) * rwkv_k_a[l])
        wkv = rwkv7_scan(heads(r_b), heads(w_b), heads(k_til), heads(v_b), kappa_hat, heads(a_b))
        bonus = jnp.sum(heads(r_b * rwkv_r_k[l] * k_til), -1, keepdims=True) * heads(v_b)
        y_b = (head_layer_norm(wkv, rwkv_gn_w[l]) + rwkv_gn_b[l] + bonus.reshape(bonus.shape[:-2] + (G,))) * silu(b_z)

        xc = causal_dwconv(c_x, lru_conv_w[l], lru_conv_b[l])
        y_c = rglru(xc, lru_w_r[l], lru_b_r[l], lru_w_i[l], lru_b_i[l], lru_lambda[l]) * silu(c_z)

        q_d = rope(heads(d_q).astype(f32), cos, sin)
        k_d = rope(heads(d_k).astype(f32), cos, sin) * (HEAD_DIM ** -0.5)
        o_d = retention_chunkwise(q_d, k_d, heads(d_v))
        y_d = head_layer_norm(o_d, ret_norm_w[l]) * silu(d_z)

        y = jnp.concatenate([y_a, y_b, y_c, y_d], axis=-1).astype(x.dtype) @ w_out[l]
        x = x + rms_norm(y, norm_post[l])
    return x
```

```python
import functools

import numpy as np
import jax
import jax.numpy as jnp
from jax import lax
from jax.experimental import pallas as pl
from jax.experimental.pallas import tpu as pltpu

F32 = jnp.float32
BF16 = jnp.bfloat16

D_MODEL = 1024
G = 512
N_HEADS = 8
HEAD_DIM = 64
N_PAIRS = N_HEADS // 2
LANES = 128
SUBLANES = 8
CONV_W = 4
LORA = 64
LRU_C = 8.0
W_DECAY_SCALE = 0.606531
ROPE_THETA = 10000.0
NORM_EPS = 1e-6
GN_EPS = 1e-5
RET_CHUNK = 64
NEG = -1e30

_A0, _AG, _B0, _C0, _D0, _END = 0, 2560, 2576, 4752, 5776, 7824
A_WIDTH = 2560 + LANES

T_MLSTM = 128
T_RET = 128
T_RWKV = 64
T_LRU = 256
TM_PROJ = 256
VMEM_LIMIT = 48 * 1024 * 1024


def _sigmoid(x):
    return jax.nn.sigmoid(x)


def _silu(x):
    return x * jax.nn.sigmoid(x)


def _softplus(x):
    return jnp.maximum(x, 0.0) + jnp.log1p(jnp.exp(-jnp.abs(x)))


def _log_sigmoid(x):
    return -_softplus(-x)


def _dot(a, b):
    return jnp.dot(a.astype(BF16), b.astype(BF16), preferred_element_type=F32)


def _dot_nt(a, b):
    return lax.dot_general(a.astype(BF16), b.astype(BF16), (((1,), (1,)), ((), ())),
                           preferred_element_type=F32)


def _dot_tn(a, b):
    return lax.dot_general(a.astype(BF16), b.astype(BF16), (((0,), (0,)), ((), ())),
                           preferred_element_type=F32)


def _shift_rows(x, prev8, j):
    xr = pltpu.roll(x, j, axis=0)
    pr = pltpu.roll(prev8, j, axis=0)
    row = lax.broadcasted_iota(jnp.int32, prev8.shape, 0)
    first = jnp.where(row < j, pr, xr[0:SUBLANES])
    return jnp.concatenate([first, xr[SUBLANES:]], axis=0)


def _cumsum(x, axis):
    n = x.shape[axis]
    idx = lax.broadcasted_iota(jnp.int32, x.shape, axis)
    d = 1
    while d < n:
        x = x + jnp.where(idx >= d, pltpu.roll(x, d, axis=axis), 0.0)
        d *= 2
    return x


def _causal_conv(x, prev8, w_ref, b_ref):
    y = x * w_ref[CONV_W - 1:CONV_W, :] + b_ref[...]
    for j in range(1, CONV_W):
        y = y + _shift_rows(x, prev8, j) * w_ref[CONV_W - 1 - j:CONV_W - j, :]
    return y


def _pair_sum(x, m_e):
    se = jnp.sum(jnp.where(m_e, x, 0.0), axis=-1, keepdims=True)
    so = jnp.sum(jnp.where(m_e, 0.0, x), axis=-1, keepdims=True)
    return jnp.where(m_e, se, so)


def _pair_layer_norm(x, m_e):
    mu = _pair_sum(x, m_e) * (1.0 / HEAD_DIM)
    xc = x - mu
    var = _pair_sum(xc * xc, m_e) * (1.0 / HEAD_DIM)
    return xc * lax.rsqrt(var + GN_EPS)


def _block_diag_mask():
    r = lax.broadcasted_iota(jnp.int32, (LANES, LANES), 0)
    c = lax.broadcasted_iota(jnp.int32, (LANES, LANES), 1)
    return (r < HEAD_DIM) == (c < HEAD_DIM)


def _rms_in(x_ref, g_ref):
    x = x_ref[...]
    ms = jnp.mean(x * x, axis=-1, keepdims=True)
    return ((x * lax.rsqrt(ms + NORM_EPS)) * g_ref[...]).astype(BF16)


def _proj_kernel(x_ref, g_ref, w_ref, o_ref):
    h = _rms_in(x_ref, g_ref)
    o_ref[...] = jnp.dot(h, w_ref[...], preferred_element_type=F32)


def _proj_gates_kernel(x_ref, g_ref, w_ref, wg_ref, o_ref, gt_ref):
    h = _rms_in(x_ref, g_ref)
    o_ref[...] = jnp.dot(h, w_ref[...], preferred_element_type=F32)
    gt_ref[...] = lax.dot_general(wg_ref[...], h, (((1,), (1,)), ((), ())),
                                  preferred_element_type=F32)


def _proj(x2, gain, w, wg=None):
    m, d = x2.shape
    n = w.shape[1]
    tm = TM_PROJ
    in_specs = [pl.BlockSpec((tm, d), lambda i: (i, 0)),
                pl.BlockSpec((1, d), lambda i: (0, 0)),
                pl.BlockSpec((d, n), lambda i: (0, 0))]
    params = pltpu.CompilerParams(dimension_semantics=("parallel",), vmem_limit_bytes=VMEM_LIMIT)
    if wg is None:
        return pl.pallas_call(
            _proj_kernel, grid=(m // tm,), in_specs=in_specs,
            out_specs=pl.BlockSpec((tm, n), lambda i: (i, 0)),
            out_shape=jax.ShapeDtypeStruct((m, n), F32), compiler_params=params,
            name=f"in_proj_{n}",
        )(x2, gain, w)
    ng = wg.shape[0]
    return pl.pallas_call(
        _proj_gates_kernel, grid=(m // tm,),
        in_specs=in_specs + [pl.BlockSpec((ng, d), lambda i: (0, 0))],
        out_specs=[pl.BlockSpec((tm, n), lambda i: (i, 0)), pl.BlockSpec((ng, tm), lambda i: (0, i))],
        out_shape=[jax.ShapeDtypeStruct((m, n), F32), jax.ShapeDtypeStruct((ng, m), F32)],
        compiler_params=params, name="in_proj_mlstm",
    )(x2, gain, w, wg)


def _out_kernel(ya_ref, yb_ref, yc_ref, yd_ref, w_ref, g_ref, x_ref, o_ref):
    acc = jnp.dot(ya_ref[...].astype(BF16), w_ref[0], preferred_element_type=F32)
    acc += jnp.dot(yb_ref[...].astype(BF16), w_ref[1], preferred_element_type=F32)
    acc += jnp.dot(yc_ref[...].astype(BF16), w_ref[2], preferred_element_type=F32)
    acc += jnp.dot(yd_ref[...].astype(BF16), w_ref[3], preferred_element_type=F32)
    ms = jnp.mean(acc * acc, axis=-1, keepdims=True)
    o_ref[...] = x_ref[...] + (acc * lax.rsqrt(ms + NORM_EPS)) * g_ref[...]


def _out_proj(ys, w4, gain, x2):
    m, d = x2.shape
    tm = TM_PROJ
    yspec = pl.BlockSpec((tm, G), lambda i: (i, 0))
    return pl.pallas_call(
        _out_kernel, grid=(m // tm,),
        in_specs=[yspec, yspec, yspec, yspec,
                  pl.BlockSpec((4, G, d), lambda i: (0, 0, 0)),
                  pl.BlockSpec((1, d), lambda i: (0, 0)),
                  pl.BlockSpec((tm, d), lambda i: (i, 0))],
        out_specs=pl.BlockSpec((tm, d), lambda i: (i, 0)),
        out_shape=jax.ShapeDtypeStruct((m, d), F32),
        compiler_params=pltpu.CompilerParams(dimension_semantics=("parallel",),
                                             vmem_limit_bytes=VMEM_LIMIT),
        name="out_proj",
    )(*ys, w4, gain, x2)


def _mlstm_kernel(p_ref, gt_ref, cw_ref, cb_ref, bcol_ref, brow_ref, nw_ref, o_ref,
                  prev_ref, c_ref, n_ref, m_ref):
    t = p_ref.shape[1]

    @pl.when(pl.program_id(1) == 0)
    def _():
        prev_ref[...] = jnp.zeros_like(prev_ref)
        c_ref[...] = jnp.zeros_like(c_ref)
        n_ref[...] = jnp.zeros_like(n_ref)
        m_ref[...] = jnp.zeros_like(m_ref)

    x_qk = p_ref[0, :, 0:2 * G]
    qk = _silu(_causal_conv(x_qk, prev_ref[...], cw_ref, cb_ref))
    prev_ref[...] = x_qk[t - SUBLANES:t, :]

    gcol = p_ref[0, :, _AG:_AG + LANES] + bcol_ref[...]
    cumc = _cumsum(_log_sigmoid(gcol), 0)
    grow = gt_ref[...] + brow_ref[...]
    i_row = grow[0:N_HEADS]
    cumr = _cumsum(_log_sigmoid(grow[N_HEADS:2 * N_HEADS]), 1)

    lane = lax.broadcasted_iota(jnp.int32, (t, LANES), 1)
    m_e = lane < HEAD_DIM
    m_e1 = lax.broadcasted_iota(jnp.int32, (1, LANES), 1) < HEAD_DIM
    tril = (lax.broadcasted_iota(jnp.int32, (t, t), 0) >= lax.broadcasted_iota(jnp.int32, (t, t), 1))
    bd = _block_diag_mask()

    for p in range(N_PAIRS):
        sl = slice(p * LANES, (p + 1) * LANES)
        q = qk[:, sl]
        k = qk[:, G + p * LANES:G + (p + 1) * LANES] * (HEAD_DIM ** -0.5)
        v = p_ref[0, :, 2 * G + p * LANES:2 * G + (p + 1) * LANES]
        kb = k.astype(BF16)
        vb = v.astype(BF16)
        c_st = c_ref[p]
        n_st = n_ref[p]
        m_st = m_ref[p]
        qn = _pair_sum(q * n_st, m_e)

        parts = []
        for half in range(2):
            h = 2 * p + half
            cum_c = cumc[:, N_HEADS + h:N_HEADS + h + 1]
            i_c = gcol[:, h:h + 1]
            cum_r = cumr[h:h + 1, :]
            i_r = i_row[h:h + 1, :]
            m_prev = m_st[:, half * HEAD_DIM:half * HEAD_DIM + 1]
            log_d = jnp.where(tril, cum_c - cum_r + i_r, NEG)
            inter = cum_c + m_prev
            m_t = jnp.maximum(inter, jnp.max(log_d, axis=-1, keepdims=True))
            qm = jnp.where(m_e, q, 0.0) if half == 0 else jnp.where(m_e, 0.0, q)
            s = _dot_nt(qm, kb) * jnp.exp(log_d - m_t)
            num = jnp.dot(s.astype(BF16), vb, preferred_element_type=F32)
            den = jnp.sum(s, axis=-1, keepdims=True)
            s_int = jnp.exp(inter - m_t)
            tot = cum_c[t - 1:t, :]
            log_w = tot - cum_c + i_c
            m_new = jnp.maximum(tot + m_prev, jnp.max(log_w, axis=0, keepdims=True))
            wj = jnp.exp(log_w - m_new)
            sc = jnp.exp(tot + m_prev - m_new)
            parts.append((num, den, s_int, m_t, wj, sc, m_new))

        pe, po = parts
        sel = lambda a, b: jnp.where(m_e, a, b)
        s_int = sel(pe[2], po[2])
        num = sel(pe[0], po[0]) + s_int * _dot_nt(q, c_st)
        den = sel(pe[1], po[1]) + s_int * qn
        m_t = sel(pe[3], po[3])
        h_out = num / jnp.maximum(jnp.abs(den), jnp.exp(-m_t))

        wj = sel(pe[4], po[4])
        sc = jnp.where(m_e1, pe[5], po[5])
        kw = k * wj
        c_ref[p] = sc * c_st + jnp.where(bd, _dot_tn(v, kw), 0.0)
        n_ref[p] = sc * n_st + jnp.sum(kw, axis=0, keepdims=True)
        m_ref[p] = jnp.where(m_e1, pe[6], po[6])

        o_gate = _sigmoid(p_ref[0, :, 3 * G + p * LANES:3 * G + (p + 1) * LANES])
        z = p_ref[0, :, 4 * G + p * LANES:4 * G + (p + 1) * LANES]
        y = _pair_layer_norm(h_out * o_gate, m_e) * nw_ref[:, sl]
        o_ref[0, :, sl] = y * _silu(z)


def _mlstm(p_a, g_t, conv_w, conv_b, i_bias, f_bias, norm_w):
    b, s, _ = p_a.shape
    t = T_MLSTM
    nc = s // t
    bcol = jnp.zeros((1, LANES), F32).at[0, 0:N_HEADS].set(i_bias).at[0, N_HEADS:2 * N_HEADS].set(f_bias)
    brow = jnp.concatenate([i_bias, f_bias]).reshape(2 * N_HEADS, 1)
    const = lambda shape: pl.BlockSpec(shape, lambda bi, ci: (0,) * len(shape))
    return pl.pallas_call(
        _mlstm_kernel, grid=(b, nc),
        in_specs=[pl.BlockSpec((1, t, A_WIDTH), lambda bi, ci: (bi, ci, 0)),
                  pl.BlockSpec((2 * N_HEADS, t), lambda bi, ci: (0, bi * nc + ci)),
                  const((CONV_W, 2 * G)), const((1, 2 * G)), const((1, LANES)),
                  const((2 * N_HEADS, 1)), const((1, G))],
        out_specs=pl.BlockSpec((1, t, G), lambda bi, ci: (bi, ci, 0)),
        out_shape=jax.ShapeDtypeStruct((b, s, G), F32),
        scratch_shapes=[pltpu.VMEM((SUBLANES, 2 * G), F32),
                        pltpu.VMEM((N_PAIRS, LANES, LANES), F32),
                        pltpu.VMEM((N_PAIRS, 1, LANES), F32),
                        pltpu.VMEM((N_PAIRS, 1, LANES), F32)],
        compiler_params=pltpu.CompilerParams(dimension_semantics=("parallel", "arbitrary"),
                                             vmem_limit_bytes=VMEM_LIMIT),
        name="mlstm",
    )(p_a, g_t, conv_w, conv_b.reshape(1, -1), bcol, brow, norm_w.reshape(1, -1))


def _stack(x, m_e):
    return jnp.concatenate([jnp.where(m_e, x, 0.0), jnp.where(m_e, 0.0, x)], axis=0)


def _rwkv_kernel(p_ref, mu_ref, wlo_ref, b0_ref, kk_ref, ka_ref, rk_ref, gw_ref, gb_ref,
                 msk_ref, o_ref, prev_ref, s_ref):
    t = p_ref.shape[1]
    sh_w = 3 * G + 2 * LORA

    @pl.when(pl.program_id(1) == 0)
    def _():
        prev_ref[...] = jnp.zeros_like(prev_ref)
        s_ref[...] = jnp.zeros_like(s_ref)

    x = p_ref[0, :, 0:sh_w]
    xs = x + mu_ref[...] * (_shift_rows(x, prev_ref[...], 1) - x)
    prev_ref[...] = x[t - SUBLANES:t, :]

    r = xs[:, 0:G]
    k = xs[:, G:2 * G]
    v = xs[:, 2 * G:3 * G]
    lo = xs[:, 3 * G:sh_w]
    lane = lax.broadcasted_iota(jnp.int32, (t, LANES), 1)
    m_e = lane < HEAD_DIM
    lo = jnp.where(m_e, jnp.tanh(lo), lo)
    pre = _dot(lo, wlo_ref[...]) + b0_ref[...]
    log_w = -W_DECAY_SCALE * _sigmoid(pre[:, 0:G])
    a = _sigmoid(pre[:, G:2 * G])

    log_p = _cumsum(log_w, 0)
    log_pl = log_p[t - 1:t, :]
    p_in = jnp.exp(log_p)
    p_inv = jnp.exp(-log_p)
    p_prev = jnp.exp(log_p - log_w)
    p_end = jnp.exp(log_pl - log_p)
    p_l = jnp.exp(log_pl)

    kappa = k * kk_ref[...]
    k_til = k * (1.0 + (a - 1.0) * ka_ref[...])
    rkk = r * rk_ref[...] * k_til
    kap2 = kappa * kappa

    strict = msk_ref[0] > 0.5
    incl = msk_ref[1] > 0.5
    eye = msk_ref[2]

    for p in range(N_PAIRS):
        sl = slice(p * LANES, (p + 1) * LANES)
        kh = kappa[:, sl] * lax.rsqrt(_pair_sum(kap2[:, sl], m_e) + 1e-12)
        b_v = a[:, sl] * kh
        kt = k_til[:, sl]
        v_p = v[:, sl]
        khs = _stack(kh * p_prev[:, sl], m_e).astype(BF16)
        rs = _stack(r[:, sl] * p_in[:, sl], m_e).astype(BF16)
        bs = _stack(b_v * p_inv[:, sl], m_e).astype(BF16)
        kts = _stack(kt * p_inv[:, sl], m_e).astype(BF16)
        vs = _stack(v_p, m_e).astype(BF16)
        kte = _stack(kt * p_end[:, sl], m_e)
        be = _stack(b_v * p_end[:, sl], m_e)

        a_ub = jnp.where(strict, _dot_nt(khs, bs), 0.0)
        a_uk = jnp.where(strict, _dot_nt(khs, kts), 0.0)
        a_rb = jnp.where(incl, _dot_nt(rs, bs), 0.0)
        a_rk = jnp.where(incl, _dot_nt(rs, kts), 0.0)

        xinv = eye - a_ub * msk_ref[3]
        for lvl in range(4, 9):
            xinv = xinv - _dot(_dot(xinv, a_ub * msk_ref[lvl]), xinv)

        s_st = s_ref[p]
        wk = _dot(xinv, khs)
        uv = _dot(xinv, _dot(a_uk, vs))
        u = -(_dot_nt(wk, s_st) + uv)
        ys = _dot_nt(rs, s_st) + _dot(a_rk, vs) + _dot(a_rb, u)
        wkv = ys[0:t] + ys[t:2 * t]
        s_ref[p] = s_st * p_l[:, sl] + _dot_tn(vs, kte) + _dot_tn(u, be)

        bonus = _pair_sum(rkk[:, sl], m_e) * v_p
        z = p_ref[0, :, sh_w + p * LANES:sh_w + (p + 1) * LANES]
        y = _pair_layer_norm(wkv, m_e) * gw_ref[:, sl] + gb_ref[:, sl] + bonus
        o_ref[0, :, sl] = y * _silu(z)


def _rwkv_masks():
    n = 2 * T_RWKV
    r = np.arange(n)[:, None]
    c = np.arange(n)[None, :]
    ms = [r > c, r >= c, r == c]
    b = 1
    while b < T_RWKV:
        ms.append((r // (2 * b) == c // (2 * b)) & (r % (2 * b) >= b) & (c % (2 * b) < b))
        b *= 2
    return jnp.asarray(np.stack(ms).astype(np.float32))


def _rwkv(p_b, mu, w_up, w0, a_up, a0, k_k, k_a, r_k, gn_w, gn_b):
    b, s, width = p_b.shape
    t = T_RWKV
    sh_w = 3 * G + 2 * LORA
    w_lo = jnp.zeros((LANES, 2 * G), F32).at[0:LORA, 0:G].set(w_up).at[LORA:, G:].set(a_up).astype(BF16)
    b0 = jnp.concatenate([w0, a0]).reshape(1, 2 * G)
    row = lambda a: a.reshape(1, -1)
    const = lambda shape: pl.BlockSpec(shape, lambda bi, ci: (0,) * len(shape))
    return pl.pallas_call(
        _rwkv_kernel, grid=(b, s // t),
        in_specs=[pl.BlockSpec((1, t, width), lambda bi, ci: (bi, ci, 0)),
                  const((1, sh_w)), const((LANES, 2 * G)), const((1, 2 * G)),
                  const((1, G)), const((1, G)), const((1, G)), const((1, G)), const((1, G)),
                  const((9, 2 * t, 2 * t))],
        out_specs=pl.BlockSpec((1, t, G), lambda bi, ci: (bi, ci, 0)),
        out_shape=jax.ShapeDtypeStruct((b, s, G), F32),
        scratch_shapes=[pltpu.VMEM((SUBLANES, sh_w), F32),
                        pltpu.VMEM((N_PAIRS, LANES, LANES), F32)],
        compiler_params=pltpu.CompilerParams(dimension_semantics=("parallel", "arbitrary"),
                                             vmem_limit_bytes=VMEM_LIMIT),
        name="rwkv7",
    )(p_b, row(mu), w_lo, b0, row(k_k), row(k_a), row(r_k), row(gn_w), row(gn_b), _rwkv_masks())


def _lru_kernel(p_ref, cw_ref, cb_ref, wr_ref, br_ref, wi_ref, bi_ref, lam_ref, o_ref,
                prev_ref, h_ref):
    t = p_ref.shape[1]

    @pl.when(pl.program_id(1) == 0)
    def _():
        prev_ref[...] = jnp.zeros_like(prev_ref)
        h_ref[...] = jnp.zeros_like(h_ref)

    x = p_ref[0, :, 0:G]
    xc = _causal_conv(x, prev_ref[...], cw_ref, cb_ref)
    prev_ref[...] = x[t - SUBLANES:t, :]

    xb = xc.astype(BF16)
    r = _sigmoid(jnp.dot(xb, wr_ref[...], preferred_element_type=F32) + br_ref[...])
    i = _sigmoid(jnp.dot(xb, wi_ref[...], preferred_element_type=F32) + bi_ref[...])
    log_a = (-LRU_C * r) * _softplus(-lam_ref[...])
    a = jnp.exp(log_a)
    th = jnp.tanh(log_a)
    u = jnp.sqrt(-2.0 * th / (1.0 - th)) * (i * xc)

    row = lax.broadcasted_iota(jnp.int32, (t, G), 0)
    d = 1
    while d < t:
        keep = row >= d
        u = u + a * jnp.where(keep, pltpu.roll(u, d, axis=0), 0.0)
        a = a * jnp.where(keep, pltpu.roll(a, d, axis=0), 1.0)
        d *= 2
    h = u + a * h_ref[...]
    h_ref[...] = h[t - 1:t, :]
    o_ref[0] = h * _silu(p_ref[0, :, G:2 * G])


def _block_diag_weight(w):
    eye = jnp.eye(N_HEADS, dtype=w.dtype)
    return jnp.einsum("hij,hg->higj", w, eye).reshape(G, G)


def _lru(p_c, conv_w, conv_b, w_r, b_r, w_i, b_i, lam):
    b, s, width = p_c.shape
    t = T_LRU
    row = lambda a: a.reshape(1, -1)
    const = lambda shape: pl.BlockSpec(shape, lambda bi, ci: (0,) * len(shape))
    return pl.pallas_call(
        _lru_kernel, grid=(b, s // t),
        in_specs=[pl.BlockSpec((1, t, width), lambda bi, ci: (bi, ci, 0)),
                  const((CONV_W, G)), const((1, G)), const((G, G)), const((1, G)),
                  const((G, G)), const((1, G)), const((1, G))],
        out_specs=pl.BlockSpec((1, t, G), lambda bi, ci: (bi, ci, 0)),
        out_shape=jax.ShapeDtypeStruct((b, s, G), F32),
        scratch_shapes=[pltpu.VMEM((SUBLANES, G), F32), pltpu.VMEM((1, G), F32)],
        compiler_params=pltpu.CompilerParams(dimension_semantics=("parallel", "arbitrary"),
                                             vmem_limit_bytes=VMEM_LIMIT),
        name="rglru",
    )(p_c, conv_w, row(conv_b), _block_diag_weight(w_r).astype(BF16), row(b_r),
      _block_diag_weight(w_i).astype(BF16), row(b_i), row(lam))


def _ret_kernel(p_ref, cos_ref, sin_ref, dm_ref, xi_ref, zeta_ref, g_ref, nw_ref, o_ref, r_ref):
    t = p_ref.shape[1]

    @pl.when(pl.program_id(1) == 0)
    def _():
        r_ref[...] = jnp.zeros_like(r_ref)

    lane = lax.broadcasted_iota(jnp.int32, (t, LANES), 1)
    m_e = lane < HEAD_DIM
    first_half = (lane % HEAD_DIM) < (HEAD_DIM // 2)
    bd = _block_diag_mask()
    cos = cos_ref[...]
    sin = sin_ref[...]

    def rope(x):
        swapped = jnp.where(first_half, pltpu.roll(x, LANES - HEAD_DIM // 2, axis=1),
                            pltpu.roll(x, HEAD_DIM // 2, axis=1))
        return x * cos + swapped * sin

    for p in range(N_PAIRS):
        sl = slice(p * LANES, (p + 1) * LANES)
        q = rope(p_ref[0, :, sl])
        k = rope(p_ref[0, :, G + p * LANES:G + (p + 1) * LANES]) * (HEAD_DIM ** -0.5)
        v = p_ref[0, :, 2 * G + p * LANES:2 * G + (p + 1) * LANES]
        kb = k.astype(BF16)
        vb = v.astype(BF16)
        r_st = r_ref[p]
        o_e = _dot(_dot_nt(jnp.where(m_e, q, 0.0), kb) * dm_ref[2 * p], vb)
        o_o = _dot(_dot_nt(jnp.where(m_e, 0.0, q), kb) * dm_ref[2 * p + 1], vb)
        o = jnp.where(m_e, o_e, o_o) + xi_ref[p] * _dot(q, r_st)
        r_ref[p] = g_ref[p] * r_st + jnp.where(bd, _dot_tn(k * zeta_ref[p], v), 0.0)
        z = p_ref[0, :, 3 * G + p * LANES:3 * G + (p + 1) * LANES]
        o_ref[0, :, sl] = _pair_layer_norm(o, m_e) * nw_ref[:, sl] * _silu(z)


def _ret_tables(s):
    t = T_RET
    half = HEAD_DIM // 2
    pos = jnp.arange(s, dtype=F32)
    inv_freq = ROPE_THETA ** (-jnp.arange(half, dtype=F32) / half)
    ang = pos[:, None] * inv_freq[None, :]
    cos = jnp.tile(jnp.cos(ang), (1, LANES // half))
    sin = jnp.sin(ang)
    sin = jnp.tile(jnp.concatenate([-sin, sin], axis=-1), (1, LANES // HEAD_DIM))
    log_g = jnp.log1p(-jnp.exp2(-5.0 - jnp.arange(N_HEADS, dtype=F32)))
    idx = jnp.arange(t, dtype=F32)
    chunk = jnp.arange(t) // RET_CHUNK
    visible = chunk[:, None] >= chunk[None, :]
    dm = jnp.where(visible[None], jnp.exp(log_g[:, None, None] * jnp.abs(idx[:, None] - idx[None, :])), 0.0)
    pair_lanes = lambda a: jnp.repeat(a.reshape(N_PAIRS, 2, -1), HEAD_DIM, axis=1).transpose(0, 2, 1)
    xi = pair_lanes(jnp.exp(log_g[:, None] * (idx + 1.0)))
    zeta = pair_lanes(jnp.exp(log_g[:, None] * (t - 1.0 - idx)))
    g_blk = pair_lanes(jnp.exp(log_g * t)[:, None])
    return cos, sin, dm, xi, zeta, g_blk


def _retention(p_d, norm_w, tables):
    b, s, width = p_d.shape
    t = T_RET
    cos, sin, dm, xi, zeta, g_blk = tables
    const = lambda shape: pl.BlockSpec(shape, lambda bi, ci: (0,) * len(shape))
    return pl.pallas_call(
        _ret_kernel, grid=(b, s // t),
        in_specs=[pl.BlockSpec((1, t, width), lambda bi, ci: (bi, ci, 0)),
                  pl.BlockSpec((t, LANES), lambda bi, ci: (ci, 0)),
                  pl.BlockSpec((t, LANES), lambda bi, ci: (ci, 0)),
                  const((N_HEADS, t, t)), const((N_PAIRS, t, LANES)), const((N_PAIRS, t, LANES)),
                  const((N_PAIRS, 1, LANES)), const((1, G))],
        out_specs=pl.BlockSpec((1, t, G), lambda bi, ci: (bi, ci, 0)),
        out_shape=jax.ShapeDtypeStruct((b, s, G), F32),
        scratch_shapes=[pltpu.VMEM((N_PAIRS, LANES, LANES), F32)],
        compiler_params=pltpu.CompilerParams(dimension_semantics=("parallel", "arbitrary"),
                                             vmem_limit_bytes=VMEM_LIMIT),
        name="retention",
    )(p_d, cos, sin, dm, xi, zeta, g_blk, norm_w.reshape(1, -1))


def kernel(x, norm_pre, norm_post, w_in, w_out, mlstm_conv_w, mlstm_conv_b, mlstm_i_bias, mlstm_f_bias, mlstm_norm_w, rwkv_mu, rwkv_w_up, rwkv_w0, rwkv_a_up, rwkv_a0, rwkv_k_k, rwkv_k_a, rwkv_r_k, rwkv_gn_w, rwkv_gn_b, lru_conv_w, lru_conv_b, lru_w_r, lru_b_r, lru_w_i, lru_b_i, lru_lambda, ret_norm_w):
    b, s, d = x.shape
    depth = w_in.shape[0]
    tables = _ret_tables(s)
    x2 = x.reshape(b * s, d)
    for l in range(depth):
        w = w_in[l]
        gain = norm_pre[l].reshape(1, d)
        w_gates = w[:, _AG:_B0]
        w_a = jnp.concatenate([w[:, _A0:_AG], w_gates, jnp.zeros((d, LANES - 2 * N_HEADS), F32)],
                              axis=1).astype(BF16)
        p_a, g_t = _proj(x2, gain, w_a, w_gates.T.astype(BF16))
        p_b = _proj(x2, gain, w[:, _B0:_C0].astype(BF16))
        p_c = _proj(x2, gain, w[:, _C0:_D0].astype(BF16))
        p_d = _proj(x2, gain, w[:, _D0:_END].astype(BF16))

        y_a = _mlstm(p_a.reshape(b, s, -1), g_t, mlstm_conv_w[l], mlstm_conv_b[l],
                     mlstm_i_bias[l], mlstm_f_bias[l], mlstm_norm_w[l])
        y_b = _rwkv(p_b.reshape(b, s, -1), rwkv_mu[l], rwkv_w_up[l], rwkv_w0[l], rwkv_a_up[l],
                    rwkv_a0[l], rwkv_k_k[l], rwkv_k_a[l], rwkv_r_k[l], rwkv_gn_w[l], rwkv_gn_b[l])
        y_c = _lru(p_c.reshape(b, s, -1), lru_conv_w[l], lru_conv_b[l], lru_w_r[l], lru_b_r[l],
                   lru_w_i[l], lru_b_i[l], lru_lambda[l])
        y_d = _retention(p_d.reshape(b, s, -1), ret_norm_w[l], tables)

        ys = [y.reshape(b * s, G) for y in (y_a, y_b, y_c, y_d)]
        x2 = _out_proj(ys, w_out[l].reshape(4, G, d).astype(BF16), norm_post[l].reshape(1, d), x2)
    return x2.reshape(b, s, d)
```

```python
import functools

import numpy as np
import jax
import jax.numpy as jnp
from jax import lax
from jax.experimental import pallas as pl
from jax.experimental.pallas import tpu as pltpu

F32 = jnp.float32
BF16 = jnp.bfloat16

D_MODEL = 1024
G = 512
N_HEADS = 8
HEAD_DIM = 64
N_PAIRS = N_HEADS // 2
LANES = 128
SUBLANES = 8
CONV_W = 4
LORA = 64
LRU_C = 8.0
W_DECAY_SCALE = 0.606531
ROPE_THETA = 10000.0
NORM_EPS = 1e-6
GN_EPS = 1e-5
RET_CHUNK = 64
NEG = -1e30

_A0, _AG, _B0, _C0, _D0, _END = 0, 2560, 2576, 4752, 5776, 7824
A_WIDTH = 2560 + LANES

T_MLSTM = 128
T_RET = 128
RWKV_CHUNK = 64
T_RWKV = 128
T_LRU = 256
TM_PROJ = 256
VMEM_LIMIT = 48 * 1024 * 1024


def _sigmoid(x):
    return jax.nn.sigmoid(x)


def _silu(x):
    return x * jax.nn.sigmoid(x)


def _softplus(x):
    return jnp.maximum(x, 0.0) + jnp.log1p(jnp.exp(-jnp.abs(x)))


def _log_sigmoid(x):
    return -_softplus(-x)


def _dot(a, b):
    return jnp.dot(a.astype(BF16), b.astype(BF16), preferred_element_type=F32)


def _dot_nt(a, b):
    return lax.dot_general(a.astype(BF16), b.astype(BF16), (((1,), (1,)), ((), ())),
                           preferred_element_type=F32)


def _dot_tn(a, b):
    return lax.dot_general(a.astype(BF16), b.astype(BF16), (((0,), (0,)), ((), ())),
                           preferred_element_type=F32)


def _shift_rows(x, prev8, j):
    xr = pltpu.roll(x, j, axis=0)
    pr = pltpu.roll(prev8, j, axis=0)
    row = lax.broadcasted_iota(jnp.int32, prev8.shape, 0)
    first = jnp.where(row < j, pr, xr[0:SUBLANES])
    return jnp.concatenate([first, xr[SUBLANES:]], axis=0)


def _cumsum(x, axis):
    n = x.shape[axis]
    idx = lax.broadcasted_iota(jnp.int32, x.shape, axis)
    d = 1
    while d < n:
        x = x + jnp.where(idx >= d, pltpu.roll(x, d, axis=axis), 0.0)
        d *= 2
    return x


def _causal_conv(x, prev8, w_ref, b_ref):
    y = x * w_ref[CONV_W - 1:CONV_W, :] + b_ref[...]
    for j in range(1, CONV_W):
        y = y + _shift_rows(x, prev8, j) * w_ref[CONV_W - 1 - j:CONV_W - j, :]
    return y


def _pair_sum(x, m_e):
    se = jnp.sum(jnp.where(m_e, x, 0.0), axis=-1, keepdims=True)
    so = jnp.sum(jnp.where(m_e, 0.0, x), axis=-1, keepdims=True)
    return jnp.where(m_e, se, so)


def _pair_layer_norm(x, m_e):
    mu = _pair_sum(x, m_e) * (1.0 / HEAD_DIM)
    xc = x - mu
    var = _pair_sum(xc * xc, m_e) * (1.0 / HEAD_DIM)
    return xc * lax.rsqrt(var + GN_EPS)


def _block_diag_mask():
    r = lax.broadcasted_iota(jnp.int32, (LANES, LANES), 0)
    c = lax.broadcasted_iota(jnp.int32, (LANES, LANES), 1)
    return (r < HEAD_DIM) == (c < HEAD_DIM)


def _rms_in(x_ref, g_ref):
    x = x_ref[...]
    ms = jnp.mean(x * x, axis=-1, keepdims=True)
    return ((x * lax.rsqrt(ms + NORM_EPS)) * g_ref[...]).astype(BF16)


def _proj_kernel(x_ref, g_ref, w_ref, o_ref):
    h = _rms_in(x_ref, g_ref)
    o_ref[...] = jnp.dot(h, w_ref[...], preferred_element_type=F32)


def _proj_gates_kernel(x_ref, g_ref, w_ref, wg_ref, o_ref, gt_ref):
    h = _rms_in(x_ref, g_ref)
    o_ref[...] = jnp.dot(h, w_ref[...], preferred_element_type=F32)
    gt_ref[...] = lax.dot_general(wg_ref[...], h, (((1,), (1,)), ((), ())),
                                  preferred_element_type=F32)


def _proj(x2, gain, w, wg=None):
    m, d = x2.shape
    n = w.shape[1]
    tm = TM_PROJ
    in_specs = [pl.BlockSpec((tm, d), lambda i: (i, 0)),
                pl.BlockSpec((1, d), lambda i: (0, 0)),
                pl.BlockSpec((d, n), lambda i: (0, 0))]
    params = pltpu.CompilerParams(dimension_semantics=("parallel",), vmem_limit_bytes=VMEM_LIMIT)
    if wg is None:
        return pl.pallas_call(
            _proj_kernel, grid=(m // tm,), in_specs=in_specs,
            out_specs=pl.BlockSpec((tm, n), lambda i: (i, 0)),
            out_shape=jax.ShapeDtypeStruct((m, n), F32), compiler_params=params,
            name=f"in_proj_{n}",
        )(x2, gain, w)
    ng = wg.shape[0]
    return pl.pallas_call(
        _proj_gates_kernel, grid=(m // tm,),
        in_specs=in_specs + [pl.BlockSpec((ng, d), lambda i: (0, 0))],
        out_specs=[pl.BlockSpec((tm, n), lambda i: (i, 0)), pl.BlockSpec((ng, tm), lambda i: (0, i))],
        out_shape=[jax.ShapeDtypeStruct((m, n), F32), jax.ShapeDtypeStruct((ng, m), F32)],
        compiler_params=params, name="in_proj_mlstm",
    )(x2, gain, w, wg)


def _out_kernel(ya_ref, yb_ref, yc_ref, yd_ref, w_ref, g_ref, x_ref, o_ref):
    acc = jnp.dot(ya_ref[...].astype(BF16), w_ref[0], preferred_element_type=F32)
    acc += jnp.dot(yb_ref[...].astype(BF16), w_ref[1], preferred_element_type=F32)
    acc += jnp.dot(yc_ref[...].astype(BF16), w_ref[2], preferred_element_type=F32)
    acc += jnp.dot(yd_ref[...].astype(BF16), w_ref[3], preferred_element_type=F32)
    ms = jnp.mean(acc * acc, axis=-1, keepdims=True)
    o_ref[...] = x_ref[...] + (acc * lax.rsqrt(ms + NORM_EPS)) * g_ref[...]


def _out_proj(ys, w4, gain, x2):
    m, d = x2.shape
    tm = TM_PROJ
    yspec = pl.BlockSpec((tm, G), lambda i: (i, 0))
    return pl.pallas_call(
        _out_kernel, grid=(m // tm,),
        in_specs=[yspec, yspec, yspec, yspec,
                  pl.BlockSpec((4, G, d), lambda i: (0, 0, 0)),
                  pl.BlockSpec((1, d), lambda i: (0, 0)),
                  pl.BlockSpec((tm, d), lambda i: (i, 0))],
        out_specs=pl.BlockSpec((tm, d), lambda i: (i, 0)),
        out_shape=jax.ShapeDtypeStruct((m, d), F32),
        compiler_params=pltpu.CompilerParams(dimension_semantics=("parallel",),
                                             vmem_limit_bytes=VMEM_LIMIT),
        name="out_proj",
    )(*ys, w4, gain, x2)


def _mlstm_kernel(p_ref, gt_ref, cw_ref, cb_ref, bcol_ref, brow_ref, nw_ref, o_ref,
                  prev_ref, c_ref, n_ref, m_ref):
    t = p_ref.shape[1]

    @pl.when(pl.program_id(1) == 0)
    def _():
        prev_ref[...] = jnp.zeros_like(prev_ref)
        c_ref[...] = jnp.zeros_like(c_ref)
        n_ref[...] = jnp.zeros_like(n_ref)
        m_ref[...] = jnp.zeros_like(m_ref)

    x_qk = p_ref[0, :, 0:2 * G]
    qk = _silu(_causal_conv(x_qk, prev_ref[...], cw_ref, cb_ref))
    prev_ref[...] = x_qk[t - SUBLANES:t, :]

    gcol = p_ref[0, :, _AG:_AG + LANES] + bcol_ref[...]
    cumc = _cumsum(_log_sigmoid(gcol), 0)
    grow = gt_ref[...] + brow_ref[...]
    i_row = grow[0:N_HEADS]
    cumr = _cumsum(_log_sigmoid(grow[N_HEADS:2 * N_HEADS]), 1)

    lane = lax.broadcasted_iota(jnp.int32, (t, LANES), 1)
    m_e = lane < HEAD_DIM
    m_e1 = lax.broadcasted_iota(jnp.int32, (1, LANES), 1) < HEAD_DIM
    tril = (lax.broadcasted_iota(jnp.int32, (t, t), 0) >= lax.broadcasted_iota(jnp.int32, (t, t), 1))
    bd = _block_diag_mask()

    pairs = range(N_PAIRS)
    heads = range(N_HEADS)
    sls = [slice(p * LANES, (p + 1) * LANES) for p in pairs]
    sel = lambda xs, p: jnp.where(m_e, xs[2 * p], xs[2 * p + 1])
    sel1 = lambda xs, p: jnp.where(m_e1, xs[2 * p], xs[2 * p + 1])

    q = [qk[:, sl] for sl in sls]
    k = [qk[:, G + p * LANES:G + (p + 1) * LANES] * (HEAD_DIM ** -0.5) for p in pairs]
    v = [p_ref[0, :, 2 * G + p * LANES:2 * G + (p + 1) * LANES] for p in pairs]
    kb = [x.astype(BF16) for x in k]
    vb = [x.astype(BF16) for x in v]
    c_st = [c_ref[p] for p in pairs]
    n_st = [n_ref[p] for p in pairs]
    m_st = [m_ref[p] for p in pairs]

    qm = [(jnp.where(m_e, q[h // 2], 0.0) if h % 2 == 0 else jnp.where(m_e, 0.0, q[h // 2])).astype(BF16)
          for h in heads]
    s_raw = [_dot_nt(qm[h], kb[h // 2]) for h in heads]
    q_c = [_dot_nt(q[p], c_st[p]) for p in pairs]

    cum_c = [cumc[:, N_HEADS + h:N_HEADS + h + 1] for h in heads]
    i_c = [gcol[:, h:h + 1] for h in heads]
    m_prev = [m_st[h // 2][:, (h % 2) * HEAD_DIM:(h % 2) * HEAD_DIM + 1] for h in heads]
    log_d = [jnp.where(tril, cum_c[h] - cumr[h:h + 1, :] + i_row[h:h + 1, :], NEG) for h in heads]
    inter = [cum_c[h] + m_prev[h] for h in heads]
    m_t = [jnp.maximum(inter[h], jnp.max(log_d[h], axis=-1, keepdims=True)) for h in heads]
    s = [s_raw[h] * jnp.exp(log_d[h] - m_t[h]) for h in heads]
    num_h = [jnp.dot(s[h].astype(BF16), vb[h // 2], preferred_element_type=F32) for h in heads]
    den_h = [jnp.sum(s[h], axis=-1, keepdims=True) for h in heads]
    s_int_h = [jnp.exp(inter[h] - m_t[h]) for h in heads]

    tot = [cum_c[h][t - 1:t, :] for h in heads]
    log_w = [tot[h] - cum_c[h] + i_c[h] for h in heads]
    m_new = [jnp.maximum(tot[h] + m_prev[h], jnp.max(log_w[h], axis=0, keepdims=True)) for h in heads]
    wj_h = [jnp.exp(log_w[h] - m_new[h]) for h in heads]
    sc_h = [jnp.exp(tot[h] + m_prev[h] - m_new[h]) for h in heads]

    kw = [k[p] * sel(wj_h, p) for p in pairs]
    vk = [_dot_tn(v[p], kw[p]) for p in pairs]
    for p in pairs:
        sc = sel1(sc_h, p)
        c_ref[p] = sc * c_st[p] + jnp.where(bd, vk[p], 0.0)
        n_ref[p] = sc * n_st[p] + jnp.sum(kw[p], axis=0, keepdims=True)
        m_ref[p] = sel1(m_new, p)

    for p, sl in enumerate(sls):
        s_int = sel(s_int_h, p)
        num = sel(num_h, p) + s_int * q_c[p]
        den = sel(den_h, p) + s_int * _pair_sum(q[p] * n_st[p], m_e)
        h_out = num / jnp.maximum(jnp.abs(den), jnp.exp(-sel(m_t, p)))
        o_gate = _sigmoid(p_ref[0, :, 3 * G + p * LANES:3 * G + (p + 1) * LANES])
        z = p_ref[0, :, 4 * G + p * LANES:4 * G + (p + 1) * LANES]
        y = _pair_layer_norm(h_out * o_gate, m_e) * nw_ref[:, sl]
        o_ref[0, :, sl] = y * _silu(z)


def _mlstm(p_a, g_t, conv_w, conv_b, i_bias, f_bias, norm_w):
    b, s, _ = p_a.shape
    t = T_MLSTM
    nc = s // t
    bcol = jnp.zeros((1, LANES), F32).at[0, 0:N_HEADS].set(i_bias).at[0, N_HEADS:2 * N_HEADS].set(f_bias)
    brow = jnp.concatenate([i_bias, f_bias]).reshape(2 * N_HEADS, 1)
    const = lambda shape: pl.BlockSpec(shape, lambda bi, ci: (0,) * len(shape))
    return pl.pallas_call(
        _mlstm_kernel, grid=(b, nc),
        in_specs=[pl.BlockSpec((1, t, A_WIDTH), lambda bi, ci: (bi, ci, 0)),
                  pl.BlockSpec((2 * N_HEADS, t), lambda bi, ci: (0, bi * nc + ci)),
                  const((CONV_W, 2 * G)), const((1, 2 * G)), const((1, LANES)),
                  const((2 * N_HEADS, 1)), const((1, G))],
        out_specs=pl.BlockSpec((1, t, G), lambda bi, ci: (bi, ci, 0)),
        out_shape=jax.ShapeDtypeStruct((b, s, G), F32),
        scratch_shapes=[pltpu.VMEM((SUBLANES, 2 * G), F32),
                        pltpu.VMEM((N_PAIRS, LANES, LANES), F32),
                        pltpu.VMEM((N_PAIRS, 1, LANES), F32),
                        pltpu.VMEM((N_PAIRS, 1, LANES), F32)],
        compiler_params=pltpu.CompilerParams(dimension_semantics=("parallel", "arbitrary"),
                                             vmem_limit_bytes=VMEM_LIMIT),
        name="mlstm",
    )(p_a, g_t, conv_w, conv_b.reshape(1, -1), bcol, brow, norm_w.reshape(1, -1))


def _stack(x, m_e):
    return jnp.concatenate([jnp.where(m_e, x, 0.0), jnp.where(m_e, 0.0, x)], axis=0)


def _rwkv_kernel(p_ref, mu_ref, wlo_ref, b0_ref, kk_ref, ka_ref, rk_ref, gw_ref, gb_ref,
                 msk_ref, o_ref, prev_ref, s_ref):
    t = p_ref.shape[1]
    sh_w = 3 * G + 2 * LORA

    @pl.when(pl.program_id(1) == 0)
    def _():
        prev_ref[...] = jnp.zeros_like(prev_ref)
        s_ref[...] = jnp.zeros_like(s_ref)

    x = p_ref[0, :, 0:sh_w]
    xs = x + mu_ref[...] * (_shift_rows(x, prev_ref[...], 1) - x)
    prev_ref[...] = x[t - SUBLANES:t, :]

    r = xs[:, 0:G]
    k = xs[:, G:2 * G]
    v = xs[:, 2 * G:3 * G]
    lo = xs[:, 3 * G:sh_w]
    lane = lax.broadcasted_iota(jnp.int32, (t, LANES), 1)
    m_e = lane < HEAD_DIM
    lo = jnp.where(m_e, jnp.tanh(lo), lo)
    pre = _dot(lo, wlo_ref[...]) + b0_ref[...]
    log_w = -W_DECAY_SCALE * _sigmoid(pre[:, 0:G])
    a = _sigmoid(pre[:, G:2 * G])

    kappa = k * kk_ref[...]
    k_til = k * (1.0 + (a - 1.0) * ka_ref[...])
    rkk = r * rk_ref[...] * k_til
    kap2 = kappa * kappa

    strict = msk_ref[0] > 0.5
    incl = msk_ref[1] > 0.5
    eye = msk_ref[2]

    ch = RWKV_CHUNK
    n2 = 2 * ch
    m_c = lax.broadcasted_iota(jnp.int32, (ch, LANES), 1) < HEAD_DIM
    pairs = range(N_PAIRS)
    chunks = range(t // ch)
    chains = [(j, p) for j in chunks for p in pairs]
    rows = [slice(j * ch, (j + 1) * ch) for j in chunks]
    sls = [slice(p * LANES, (p + 1) * LANES) for p in pairs]
    blk = lambda arr, c: arr[rows[c[0]], sls[c[1]]]

    log_p = [_cumsum(log_w[rw], 0) for rw in rows]
    p_in = [jnp.exp(lp) for lp in log_p]
    p_inv = [jnp.exp(-lp) for lp in log_p]
    p_prev = [jnp.exp(lp - log_w[rw]) for lp, rw in zip(log_p, rows)]
    p_end = [jnp.exp(lp[ch - 1:ch] - lp) for lp in log_p]
    p_l = [jnp.exp(lp[ch - 1:ch]) for lp in log_p]
    dec = lambda arr, c: arr[c[0]][:, sls[c[1]]]

    kh = [blk(kappa, c) * lax.rsqrt(_pair_sum(blk(kap2, c), m_c) + 1e-12) for c in chains]
    b_v = [blk(a, c) * kh[i] for i, c in enumerate(chains)]
    stk = lambda val: _stack(val, m_c).astype(BF16)
    khs = [stk(kh[i] * dec(p_prev, c)) for i, c in enumerate(chains)]
    rs = [stk(blk(r, c) * dec(p_in, c)) for c in chains]
    bs = [stk(b_v[i] * dec(p_inv, c)) for i, c in enumerate(chains)]
    kts = [stk(blk(k_til, c) * dec(p_inv, c)) for c in chains]
    vs = [stk(blk(v, c)) for c in chains]
    kte = [stk(blk(k_til, c) * dec(p_end, c)) for c in chains]
    be = [stk(b_v[i] * dec(p_end, c)) for i, c in enumerate(chains)]
    idx = range(len(chains))

    g = [_dot_nt(jnp.concatenate([khs[i], rs[i]], axis=0), jnp.concatenate([bs[i], kts[i]], axis=0))
         for i in idx]
    a_ub = [jnp.where(strict, g[i][0:n2, 0:n2], 0.0) for i in idx]
    a_k = [jnp.concatenate([jnp.where(strict, g[i][0:n2, n2:], 0.0),
                            jnp.where(incl, g[i][n2:, n2:], 0.0)], axis=0).astype(BF16) for i in idx]
    a_rb = [jnp.where(incl, g[i][n2:, 0:n2], 0.0).astype(BF16) for i in idx]
    av = [_dot(a_k[i], vs[i]) for i in idx]

    xinv = [eye - a_ub[i] * msk_ref[3] for i in idx]
    for lvl in range(4, 9):
        xa = [_dot(xinv[i], a_ub[i] * msk_ref[lvl]) for i in idx]
        xinv = [xinv[i] - _dot(xa[i], xinv[i]) for i in idx]
    wu = [_dot(xinv[i], jnp.concatenate([khs[i], av[i][0:n2].astype(BF16)], axis=1)) for i in idx]

    s_st = [s_ref[p] for p in pairs]
    for j in chunks:
        ids = [j * N_PAIRS + p for p in pairs]
        ws = [_dot_nt(jnp.concatenate([wu[i][:, 0:LANES].astype(BF16), rs[i]], axis=0), s_st[p])
              for p, i in enumerate(ids)]
        ub = [(-(ws[p][0:n2] + wu[i][:, LANES:])).astype(BF16) for p, i in enumerate(ids)]
        ys = [ws[p][n2:] + av[i][n2:] + _dot(a_rb[i], ub[p]) for p, i in enumerate(ids)]
        s_st = [s_st[p] * p_l[j][:, sls[p]] + _dot_tn(jnp.concatenate([vs[i], ub[p]], axis=0),
                                                      jnp.concatenate([kte[i], be[i]], axis=0))
                for p, i in enumerate(ids)]
        for p, i in enumerate(ids):
            c = chains[i]
            wkv = ys[p][0:ch] + ys[p][ch:n2]
            bonus = _pair_sum(blk(rkk, c), m_c) * blk(v, c)
            z = p_ref[0, rows[j], sh_w + p * LANES:sh_w + (p + 1) * LANES]
            y = _pair_layer_norm(wkv, m_c) * gw_ref[:, sls[p]] + gb_ref[:, sls[p]] + bonus
            o_ref[0, rows[j], sls[p]] = y * _silu(z)
    for p in pairs:
        s_ref[p] = s_st[p]


def _rwkv_masks():
    n = 2 * RWKV_CHUNK
    r = np.arange(n)[:, None]
    c = np.arange(n)[None, :]
    ms = [r > c, r >= c, r == c]
    b = 1
    while b < RWKV_CHUNK:
        ms.append((r // (2 * b) == c // (2 * b)) & (r % (2 * b) >= b) & (c % (2 * b) < b))
        b *= 2
    return jnp.asarray(np.stack(ms).astype(np.float32))


def _rwkv(p_b, mu, w_up, w0, a_up, a0, k_k, k_a, r_k, gn_w, gn_b):
    b, s, width = p_b.shape
    t = T_RWKV
    sh_w = 3 * G + 2 * LORA
    w_lo = jnp.zeros((LANES, 2 * G), F32).at[0:LORA, 0:G].set(w_up).at[LORA:, G:].set(a_up).astype(BF16)
    b0 = jnp.concatenate([w0, a0]).reshape(1, 2 * G)
    row = lambda a: a.reshape(1, -1)
    const = lambda shape: pl.BlockSpec(shape, lambda bi, ci: (0,) * len(shape))
    return pl.pallas_call(
        _rwkv_kernel, grid=(b, s // t),
        in_specs=[pl.BlockSpec((1, t, width), lambda bi, ci: (bi, ci, 0)),
                  const((1, sh_w)), const((LANES, 2 * G)), const((1, 2 * G)),
                  const((1, G)), const((1, G)), const((1, G)), const((1, G)), const((1, G)),
                  const((9, 2 * RWKV_CHUNK, 2 * RWKV_CHUNK))],
        out_specs=pl.BlockSpec((1, t, G), lambda bi, ci: (bi, ci, 0)),
        out_shape=jax.ShapeDtypeStruct((b, s, G), F32),
        scratch_shapes=[pltpu.VMEM((SUBLANES, sh_w), F32),
                        pltpu.VMEM((N_PAIRS, LANES, LANES), F32)],
        compiler_params=pltpu.CompilerParams(dimension_semantics=("parallel", "arbitrary"),
                                             vmem_limit_bytes=VMEM_LIMIT),
        name="rwkv7",
    )(p_b, row(mu), w_lo, b0, row(k_k), row(k_a), row(r_k), row(gn_w), row(gn_b), _rwkv_masks())


def _lru_kernel(p_ref, cw_ref, cb_ref, wr_ref, br_ref, wi_ref, bi_ref, lam_ref, o_ref,
                prev_ref, h_ref):
    t = p_ref.shape[1]

    @pl.when(pl.program_id(1) == 0)
    def _():
        prev_ref[...] = jnp.zeros_like(prev_ref)
        h_ref[...] = jnp.zeros_like(h_ref)

    x = p_ref[0, :, 0:G]
    xc = _causal_conv(x, prev_ref[...], cw_ref, cb_ref)
    prev_ref[...] = x[t - SUBLANES:t, :]

    xb = xc.astype(BF16)
    r = _sigmoid(jnp.dot(xb, wr_ref[...], preferred_element_type=F32) + br_ref[...])
    i = _sigmoid(jnp.dot(xb, wi_ref[...], preferred_element_type=F32) + bi_ref[...])
    log_a = (-LRU_C * r) * _softplus(-lam_ref[...])
    a = jnp.exp(log_a)
    th = jnp.tanh(log_a)
    u = jnp.sqrt(-2.0 * th / (1.0 - th)) * (i * xc)

    row = lax.broadcasted_iota(jnp.int32, (t, G), 0)
    d = 1
    while d < t:
        keep = row >= d
        u = u + a * jnp.where(keep, pltpu.roll(u, d, axis=0), 0.0)
        a = a * jnp.where(keep, pltpu.roll(a, d, axis=0), 1.0)
        d *= 2
    h = u + a * h_ref[...]
    h_ref[...] = h[t - 1:t, :]
    o_ref[0] = h * _silu(p_ref[0, :, G:2 * G])


def _block_diag_weight(w):
    eye = jnp.eye(N_HEADS, dtype=w.dtype)
    return jnp.einsum("hij,hg->higj", w, eye).reshape(G, G)


def _lru(p_c, conv_w, conv_b, w_r, b_r, w_i, b_i, lam):
    b, s, width = p_c.shape
    t = T_LRU
    row = lambda a: a.reshape(1, -1)
    const = lambda shape: pl.BlockSpec(shape, lambda bi, ci: (0,) * len(shape))
    return pl.pallas_call(
        _lru_kernel, grid=(b, s // t),
        in_specs=[pl.BlockSpec((1, t, width), lambda bi, ci: (bi, ci, 0)),
                  const((CONV_W, G)), const((1, G)), const((G, G)), const((1, G)),
                  const((G, G)), const((1, G)), const((1, G))],
        out_specs=pl.BlockSpec((1, t, G), lambda bi, ci: (bi, ci, 0)),
        out_shape=jax.ShapeDtypeStruct((b, s, G), F32),
        scratch_shapes=[pltpu.VMEM((SUBLANES, G), F32), pltpu.VMEM((1, G), F32)],
        compiler_params=pltpu.CompilerParams(dimension_semantics=("parallel", "arbitrary"),
                                             vmem_limit_bytes=VMEM_LIMIT),
        name="rglru",
    )(p_c, conv_w, row(conv_b), _block_diag_weight(w_r).astype(BF16), row(b_r),
      _block_diag_weight(w_i).astype(BF16), row(b_i), row(lam))


def _ret_kernel(p_ref, cos_ref, sin_ref, dm_ref, xi_ref, zeta_ref, g_ref, nw_ref, o_ref, r_ref):
    t = p_ref.shape[1]

    @pl.when(pl.program_id(1) == 0)
    def _():
        r_ref[...] = jnp.zeros_like(r_ref)

    lane = lax.broadcasted_iota(jnp.int32, (t, LANES), 1)
    m_e = lane < HEAD_DIM
    first_half = (lane % HEAD_DIM) < (HEAD_DIM // 2)
    bd = _block_diag_mask()
    cos = cos_ref[...]
    sin = sin_ref[...]

    def rope(x):
        swapped = jnp.where(first_half, pltpu.roll(x, LANES - HEAD_DIM // 2, axis=1),
                            pltpu.roll(x, HEAD_DIM // 2, axis=1))
        return x * cos + swapped * sin

    pairs = range(N_PAIRS)
    heads = range(N_HEADS)
    sls = [slice(p * LANES, (p + 1) * LANES) for p in pairs]
    q = [rope(p_ref[0, :, sl]) for sl in sls]
    k = [rope(p_ref[0, :, G + p * LANES:G + (p + 1) * LANES]) * (HEAD_DIM ** -0.5) for p in pairs]
    v = [p_ref[0, :, 2 * G + p * LANES:2 * G + (p + 1) * LANES] for p in pairs]
    kb = [x.astype(BF16) for x in k]
    vb = [x.astype(BF16) for x in v]
    r_st = [r_ref[p] for p in pairs]
    qm = [(jnp.where(m_e, q[h // 2], 0.0) if h % 2 == 0 else jnp.where(m_e, 0.0, q[h // 2])).astype(BF16)
          for h in heads]
    s = [_dot_nt(qm[h], kb[h // 2]) * dm_ref[h] for h in heads]
    q_r = [_dot(q[p], r_st[p]) for p in pairs]
    o_h = [_dot(s[h], vb[h // 2]) for h in heads]
    kv = [_dot_tn(k[p] * zeta_ref[p], v[p]) for p in pairs]
    for p, sl in enumerate(sls):
        r_ref[p] = g_ref[p] * r_st[p] + jnp.where(bd, kv[p], 0.0)
        o = jnp.where(m_e, o_h[2 * p], o_h[2 * p + 1]) + xi_ref[p] * q_r[p]
        z = p_ref[0, :, 3 * G + p * LANES:3 * G + (p + 1) * LANES]
        o_ref[0, :, sl] = _pair_layer_norm(o, m_e) * nw_ref[:, sl] * _silu(z)


def _ret_tables(s):
    t = T_RET
    half = HEAD_DIM // 2
    pos = jnp.arange(s, dtype=F32)
    inv_freq = ROPE_THETA ** (-jnp.arange(half, dtype=F32) / half)
    ang = pos[:, None] * inv_freq[None, :]
    cos = jnp.tile(jnp.cos(ang), (1, LANES // half))
    sin = jnp.sin(ang)
    sin = jnp.tile(jnp.concatenate([-sin, sin], axis=-1), (1, LANES // HEAD_DIM))
    log_g = jnp.log1p(-jnp.exp2(-5.0 - jnp.arange(N_HEADS, dtype=F32)))
    idx = jnp.arange(t, dtype=F32)
    chunk = jnp.arange(t) // RET_CHUNK
    visible = chunk[:, None] >= chunk[None, :]
    dm = jnp.where(visible[None], jnp.exp(log_g[:, None, None] * jnp.abs(idx[:, None] - idx[None, :])), 0.0)
    pair_lanes = lambda a: jnp.repeat(a.reshape(N_PAIRS, 2, -1), HEAD_DIM, axis=1).transpose(0, 2, 1)
    xi = pair_lanes(jnp.exp(log_g[:, None] * (idx + 1.0)))
    zeta = pair_lanes(jnp.exp(log_g[:, None] * (t - 1.0 - idx)))
    g_blk = pair_lanes(jnp.exp(log_g * t)[:, None])
    return cos, sin, dm, xi, zeta, g_blk


def _retention(p_d, norm_w, tables):
    b, s, width = p_d.shape
    t = T_RET
    cos, sin, dm, xi, zeta, g_blk = tables
    const = lambda shape: pl.BlockSpec(shape, lambda bi, ci: (0,) * len(shape))
    return pl.pallas_call(
        _ret_kernel, grid=(b, s // t),
        in_specs=[pl.BlockSpec((1, t, width), lambda bi, ci: (bi, ci, 0)),
                  pl.BlockSpec((t, LANES), lambda bi, ci: (ci, 0)),
                  pl.BlockSpec((t, LANES), lambda bi, ci: (ci, 0)),
                  const((N_HEADS, t, t)), const((N_PAIRS, t, LANES)), const((N_PAIRS, t, LANES)),
                  const((N_PAIRS, 1, LANES)), const((1, G))],
        out_specs=pl.BlockSpec((1, t, G), lambda bi, ci: (bi, ci, 0)),
        out_shape=jax.ShapeDtypeStruct((b, s, G), F32),
        scratch_shapes=[pltpu.VMEM((N_PAIRS, LANES, LANES), F32)],
        compiler_params=pltpu.CompilerParams(dimension_semantics=("parallel", "arbitrary"),
                                             vmem_limit_bytes=VMEM_LIMIT),
        name="retention",
    )(p_d, cos, sin, dm, xi, zeta, g_blk, norm_w.reshape(1, -1))


def kernel(x, norm_pre, norm_post, w_in, w_out, mlstm_conv_w, mlstm_conv_b, mlstm_i_bias, mlstm_f_bias, mlstm_norm_w, rwkv_mu, rwkv_w_up, rwkv_w0, rwkv_a_up, rwkv_a0, rwkv_k_k, rwkv_k_a, rwkv_r_k, rwkv_gn_w, rwkv_gn_b, lru_conv_w, lru_conv_b, lru_w_r, lru_b_r, lru_w_i, lru_b_i, lru_lambda, ret_norm_w):
    b, s, d = x.shape
    depth = w_in.shape[0]
    tables = _ret_tables(s)
    x2 = x.reshape(b * s, d)
    for l in range(depth):
        w = w_in[l]
        gain = norm_pre[l].reshape(1, d)
        w_gates = w[:, _AG:_B0]
        w_a = jnp.concatenate([w[:, _A0:_AG], w_gates, jnp.zeros((d, LANES - 2 * N_HEADS), F32)],
                              axis=1).astype(BF16)
        p_a, g_t = _proj(x2, gain, w_a, w_gates.T.astype(BF16))
        p_b = _proj(x2, gain, w[:, _B0:_C0].astype(BF16))
        p_c = _proj(x2, gain, w[:, _C0:_D0].astype(BF16))
        p_d = _proj(x2, gain, w[:, _D0:_END].astype(BF16))

        y_a = _mlstm(p_a.reshape(b, s, -1), g_t, mlstm_conv_w[l], mlstm_conv_b[l],
                     mlstm_i_bias[l], mlstm_f_bias[l], mlstm_norm_w[l])
        y_b = _rwkv(p_b.reshape(b, s, -1), rwkv_mu[l], rwkv_w_up[l], rwkv_w0[l], rwkv_a_up[l],
                    rwkv_a0[l], rwkv_k_k[l], rwkv_k_a[l], rwkv_r_k[l], rwkv_gn_w[l], rwkv_gn_b[l])
        y_c = _lru(p_c.reshape(b, s, -1), lru_conv_w[l], lru_conv_b[l], lru_w_r[l], lru_b_r[l],
                   lru_w_i[l], lru_b_i[l], lru_lambda[l])
        y_d = _retention(p_d.reshape(b, s, -1), ret_norm_w[l], tables)

        ys = [y.reshape(b * s, G) for y in (y_a, y_b, y_c, y_d)]
        x2 = _out_proj(ys, w_out[l].reshape(4, G, d).astype(BF16), norm_post[l].reshape(1, d), x2)
    return x2.reshape(b, s, d)
```

```python
import functools

import numpy as np
import jax
import jax.numpy as jnp
from jax import lax
from jax.experimental import pallas as pl
from jax.experimental.pallas import tpu as pltpu

F32 = jnp.float32
BF16 = jnp.bfloat16

D_MODEL = 1024
G = 512
N_HEADS = 8
HEAD_DIM = 64
N_PAIRS = N_HEADS // 2
LANES = 128
SUBLANES = 8
CONV_W = 4
LORA = 64
LRU_C = 8.0
W_DECAY_SCALE = 0.606531
ROPE_THETA = 10000.0
NORM_EPS = 1e-6
GN_EPS = 1e-5
RET_CHUNK = 64
NEG = -1e30

_A0, _AG, _B0, _C0, _D0, _END = 0, 2560, 2576, 4752, 5776, 7824

T_MLSTM = 128
T_RET = 128
RWKV_CHUNK = 64
T_RWKV = 128
T_LRU = 256
TM_PROJ = 512
VMEM_LIMIT = 48 * 1024 * 1024


def _sigmoid(x):
    return jax.nn.sigmoid(x)


def _silu(x):
    return x * jax.nn.sigmoid(x)


def _softplus(x):
    return jnp.maximum(x, 0.0) + jnp.log1p(jnp.exp(-jnp.abs(x)))


def _log_sigmoid(x):
    return -_softplus(-x)


def _dot(a, b):
    return jnp.dot(a.astype(BF16), b.astype(BF16), preferred_element_type=F32)


def _dot_nt(a, b):
    return lax.dot_general(a.astype(BF16), b.astype(BF16), (((1,), (1,)), ((), ())),
                           preferred_element_type=F32)


def _dot_tn(a, b):
    return lax.dot_general(a.astype(BF16), b.astype(BF16), (((0,), (0,)), ((), ())),
                           preferred_element_type=F32)


def _shift_rows(x, prev8, j):
    xr = pltpu.roll(x, j, axis=0)
    pr = pltpu.roll(prev8, j, axis=0)
    row = lax.broadcasted_iota(jnp.int32, prev8.shape, 0)
    first = jnp.where(row < j, pr, xr[0:SUBLANES])
    return jnp.concatenate([first, xr[SUBLANES:]], axis=0)


def _cumsum(x, axis):
    n = x.shape[axis]
    idx = lax.broadcasted_iota(jnp.int32, x.shape, axis)
    d = 1
    while d < n:
        x = x + jnp.where(idx >= d, pltpu.roll(x, d, axis=axis), 0.0)
        d *= 2
    return x


def _causal_conv(x, prev8, w_ref, b_ref):
    y = x * w_ref[CONV_W - 1:CONV_W, :] + b_ref[...]
    for j in range(1, CONV_W):
        y = y + _shift_rows(x, prev8, j) * w_ref[CONV_W - 1 - j:CONV_W - j, :]
    return y


def _split2(x):
    hi = x.astype(BF16)
    lo = (x - hi.astype(F32)).astype(BF16)
    return jnp.concatenate([hi, lo], axis=1)


def _head_ones():
    r = lax.broadcasted_iota(jnp.int32, (2 * LANES, LANES), 0) % LANES
    c = lax.broadcasted_iota(jnp.int32, (2 * LANES, LANES), 1)
    return ((r < HEAD_DIM) == (c < HEAD_DIM)).astype(BF16)


def _pair_sum(x, ones2):
    return jnp.dot(_split2(x), ones2, preferred_element_type=F32)


def _pair_sum_lanes(x, m_e):
    se = jnp.sum(jnp.where(m_e, x, 0.0), axis=-1, keepdims=True)
    so = jnp.sum(jnp.where(m_e, 0.0, x), axis=-1, keepdims=True)
    return jnp.where(m_e, se, so)


def _pair_layer_norm(x, ones2):
    mu = _pair_sum(x, ones2) * (1.0 / HEAD_DIM)
    xc = x - mu
    var = _pair_sum(xc * xc, ones2) * (1.0 / HEAD_DIM)
    return xc * lax.rsqrt(var + GN_EPS)


def _block_diag_mask():
    r = lax.broadcasted_iota(jnp.int32, (LANES, LANES), 0)
    c = lax.broadcasted_iota(jnp.int32, (LANES, LANES), 1)
    return (r < HEAD_DIM) == (c < HEAD_DIM)


def _proj_kernel(x_ref, g_ref, w_ref, *o_refs):
    x = x_ref[...]
    rs = lax.rsqrt(jnp.mean(x * x, axis=-1, keepdims=True) + NORM_EPS)
    acc = jnp.dot((x * g_ref[...]).astype(BF16), w_ref[...], preferred_element_type=F32) * rs
    off = 0
    for o_ref in o_refs:
        n = o_ref.shape[1]
        o_ref[...] = acc[:, off:off + n].astype(o_ref.dtype)
        off += n


def _proj(x2, gain, w, outs, name):
    m, d = x2.shape
    n = w.shape[1]
    tm = TM_PROJ
    res = pl.pallas_call(
        _proj_kernel, grid=(m // tm,),
        in_specs=[pl.BlockSpec((tm, d), lambda i: (i, 0)),
                  pl.BlockSpec((1, d), lambda i: (0, 0)),
                  pl.BlockSpec((d, n), lambda i: (0, 0))],
        out_specs=[pl.BlockSpec((tm, wd), lambda i: (i, 0)) for wd, _ in outs],
        out_shape=[jax.ShapeDtypeStruct((m, wd), dt) for wd, dt in outs],
        compiler_params=pltpu.CompilerParams(dimension_semantics=("parallel",),
                                             vmem_limit_bytes=VMEM_LIMIT),
        name=name,
    )(x2, gain, w)
    return res


def _out_kernel(ya_ref, yb_ref, yc_ref, yd_ref, w_ref, g_ref, x_ref, o_ref):
    acc = jnp.dot(ya_ref[...].astype(BF16), w_ref[0], preferred_element_type=F32)
    acc += jnp.dot(yb_ref[...].astype(BF16), w_ref[1], preferred_element_type=F32)
    acc += jnp.dot(yc_ref[...].astype(BF16), w_ref[2], preferred_element_type=F32)
    acc += jnp.dot(yd_ref[...].astype(BF16), w_ref[3], preferred_element_type=F32)
    ms = jnp.mean(acc * acc, axis=-1, keepdims=True)
    o_ref[...] = x_ref[...] + (acc * lax.rsqrt(ms + NORM_EPS)) * g_ref[...]


def _out_proj(ys, w4, gain, x2):
    m, d = x2.shape
    tm = TM_PROJ
    yspec = pl.BlockSpec((tm, G), lambda i: (i, 0))
    return pl.pallas_call(
        _out_kernel, grid=(m // tm,),
        in_specs=[yspec, yspec, yspec, yspec,
                  pl.BlockSpec((4, G, d), lambda i: (0, 0, 0)),
                  pl.BlockSpec((1, d), lambda i: (0, 0)),
                  pl.BlockSpec((tm, d), lambda i: (i, 0))],
        out_specs=pl.BlockSpec((tm, d), lambda i: (i, 0)),
        out_shape=jax.ShapeDtypeStruct((m, d), F32),
        compiler_params=pltpu.CompilerParams(dimension_semantics=("parallel",),
                                             vmem_limit_bytes=VMEM_LIMIT),
        name="out_proj",
    )(*ys, w4, gain, x2)


def _mlstm_kernel(p_ref, g_ref, cw_ref, cb_ref, bcol_ref, nw_ref, o_ref,
                  prev_ref, c_ref, n_ref, m_ref):
    t = p_ref.shape[1]

    @pl.when(pl.program_id(1) == 0)
    def _():
        prev_ref[...] = jnp.zeros_like(prev_ref)
        c_ref[...] = jnp.zeros_like(c_ref)
        n_ref[...] = jnp.zeros_like(n_ref)
        m_ref[...] = jnp.zeros_like(m_ref)

    x_qk = p_ref[0, :, 0:2 * G].astype(F32)
    qk = _silu(_causal_conv(x_qk, prev_ref[...], cw_ref, cb_ref))
    prev_ref[...] = x_qk[t - SUBLANES:t, :]

    lane = lax.broadcasted_iota(jnp.int32, (t, LANES), 1)
    m_e = lane < HEAD_DIM
    ones2 = _head_ones()
    ones_row = jnp.ones((2 * t, LANES), BF16)

    gcol = g_ref[0] + bcol_ref[...]
    cumc = _cumsum(_log_sigmoid(gcol), 0)
    grow = jnp.where(lane < N_HEADS, gcol, cumc).T
    i_row = grow[0:N_HEADS]
    cumr = grow[N_HEADS:2 * N_HEADS]

    m_e1 = lax.broadcasted_iota(jnp.int32, (1, LANES), 1) < HEAD_DIM
    tril = (lax.broadcasted_iota(jnp.int32, (t, t), 0) >= lax.broadcasted_iota(jnp.int32, (t, t), 1))
    bd = _block_diag_mask()

    pairs = range(N_PAIRS)
    heads = range(N_HEADS)
    sls = [slice(p * LANES, (p + 1) * LANES) for p in pairs]
    sel = lambda xs, p: jnp.where(m_e, xs[2 * p], xs[2 * p + 1])
    sel1 = lambda xs, p: jnp.where(m_e1, xs[2 * p], xs[2 * p + 1])

    q = [qk[:, sl] for sl in sls]
    k = [qk[:, G + p * LANES:G + (p + 1) * LANES] * (HEAD_DIM ** -0.5) for p in pairs]
    v = [p_ref[0, :, 2 * G + p * LANES:2 * G + (p + 1) * LANES] for p in pairs]
    kb = [x.astype(BF16) for x in k]
    vb = [x.astype(BF16) for x in v]
    c_st = [c_ref[p] for p in pairs]
    n_st = [n_ref[p] for p in pairs]
    m_st = [m_ref[p] for p in pairs]

    qm = [(jnp.where(m_e, q[h // 2], 0.0) if h % 2 == 0 else jnp.where(m_e, 0.0, q[h // 2])).astype(BF16)
          for h in heads]
    s_raw = [_dot_nt(qm[h], kb[h // 2]) for h in heads]
    q_c = [_dot_nt(q[p], c_st[p]) for p in pairs]

    cum_c = [cumc[:, N_HEADS + h:N_HEADS + h + 1] for h in heads]
    i_c = [gcol[:, h:h + 1] for h in heads]
    m_prev = [m_st[h // 2][:, (h % 2) * HEAD_DIM:(h % 2) * HEAD_DIM + 1] for h in heads]
    log_d = [jnp.where(tril, cum_c[h] - cumr[h:h + 1, :] + i_row[h:h + 1, :], NEG) for h in heads]
    inter = [cum_c[h] + m_prev[h] for h in heads]
    m_t = [jnp.maximum(inter[h], jnp.max(log_d[h], axis=-1, keepdims=True)) for h in heads]
    s = [s_raw[h] * jnp.exp(log_d[h] - m_t[h]) for h in heads]
    s2 = [_split2(s[h]) for h in heads]
    num_h = [jnp.dot(s2[h][:, 0:t], vb[h // 2], preferred_element_type=F32) for h in heads]
    den_h = [jnp.dot(s2[h], ones_row, preferred_element_type=F32) for h in heads]
    s_int_h = [jnp.exp(inter[h] - m_t[h]) for h in heads]

    tot = [cum_c[h][t - 1:t, :] for h in heads]
    log_w = [tot[h] - cum_c[h] + i_c[h] for h in heads]
    m_new = [jnp.maximum(tot[h] + m_prev[h], jnp.max(log_w[h], axis=0, keepdims=True)) for h in heads]
    wj_h = [jnp.exp(log_w[h] - m_new[h]) for h in heads]
    sc_h = [jnp.exp(tot[h] + m_prev[h] - m_new[h]) for h in heads]

    kw = [k[p] * sel(wj_h, p) for p in pairs]
    vk = [_dot_tn(v[p], kw[p]) for p in pairs]
    for p in pairs:
        sc = sel1(sc_h, p)
        c_ref[p] = sc * c_st[p] + jnp.where(bd, vk[p], 0.0)
        n_ref[p] = sc * n_st[p] + jnp.sum(kw[p], axis=0, keepdims=True)
        m_ref[p] = sel1(m_new, p)

    for p, sl in enumerate(sls):
        s_int = sel(s_int_h, p)
        num = sel(num_h, p) + s_int * q_c[p]
        den = sel(den_h, p) + s_int * _pair_sum(q[p] * n_st[p], ones2)
        h_out = num / jnp.maximum(jnp.abs(den), jnp.exp(-sel(m_t, p)))
        o_gate = _sigmoid(p_ref[0, :, 3 * G + p * LANES:3 * G + (p + 1) * LANES].astype(F32))
        z = p_ref[0, :, 4 * G + p * LANES:4 * G + (p + 1) * LANES].astype(F32)
        y = _pair_layer_norm(h_out * o_gate, ones2) * nw_ref[:, sl]
        o_ref[0, :, sl] = (y * _silu(z)).astype(o_ref.dtype)


def _mlstm(p_a, gates, conv_w, conv_b, i_bias, f_bias, norm_w):
    b, s, width = p_a.shape
    t = T_MLSTM
    nc = s // t
    bcol = jnp.zeros((1, LANES), F32).at[0, 0:N_HEADS].set(i_bias).at[0, N_HEADS:2 * N_HEADS].set(f_bias)
    const = lambda shape: pl.BlockSpec(shape, lambda bi, ci: (0,) * len(shape))
    return pl.pallas_call(
        _mlstm_kernel, grid=(b, nc),
        in_specs=[pl.BlockSpec((1, t, width), lambda bi, ci: (bi, ci, 0)),
                  pl.BlockSpec((1, t, LANES), lambda bi, ci: (bi, ci, 0)),
                  const((CONV_W, 2 * G)), const((1, 2 * G)), const((1, LANES)), const((1, G))],
        out_specs=pl.BlockSpec((1, t, G), lambda bi, ci: (bi, ci, 0)),
        out_shape=jax.ShapeDtypeStruct((b, s, G), BF16),
        scratch_shapes=[pltpu.VMEM((SUBLANES, 2 * G), F32),
                        pltpu.VMEM((N_PAIRS, LANES, LANES), F32),
                        pltpu.VMEM((N_PAIRS, 1, LANES), F32),
                        pltpu.VMEM((N_PAIRS, 1, LANES), F32)],
        compiler_params=pltpu.CompilerParams(dimension_semantics=("parallel", "arbitrary"),
                                             vmem_limit_bytes=VMEM_LIMIT),
        name="mlstm",
    )(p_a, gates, conv_w, conv_b.reshape(1, -1), bcol, norm_w.reshape(1, -1))


def _stack(x, m_e):
    return jnp.concatenate([jnp.where(m_e, x, 0.0), jnp.where(m_e, 0.0, x)], axis=0)


def _rwkv_kernel(p_ref, mu_ref, wlo_ref, b0_ref, kk_ref, ka_ref, rk_ref, gw_ref, gb_ref,
                 msk_ref, o_ref, prev_ref, s_ref):
    t = p_ref.shape[1]
    sh_w = 3 * G + 2 * LORA

    @pl.when(pl.program_id(1) == 0)
    def _():
        prev_ref[...] = jnp.zeros_like(prev_ref)
        s_ref[...] = jnp.zeros_like(s_ref)

    x = p_ref[0, :, 0:sh_w].astype(F32)
    xs = x + mu_ref[...] * (_shift_rows(x, prev_ref[...], 1) - x)
    prev_ref[...] = x[t - SUBLANES:t, :]

    r = xs[:, 0:G]
    k = xs[:, G:2 * G]
    v = xs[:, 2 * G:3 * G]
    lo = xs[:, 3 * G:sh_w]
    lane = lax.broadcasted_iota(jnp.int32, (t, LANES), 1)
    m_e = lane < HEAD_DIM
    lo = jnp.where(m_e, jnp.tanh(lo), lo)
    pre = _dot(lo, wlo_ref[...]) + b0_ref[...]
    log_w = -W_DECAY_SCALE * _sigmoid(pre[:, 0:G])
    a = _sigmoid(pre[:, G:2 * G])

    kappa = k * kk_ref[...]
    k_til = k * (1.0 + (a - 1.0) * ka_ref[...])
    rkk = r * rk_ref[...] * k_til
    kap2 = kappa * kappa

    strict = msk_ref[0] > 0.5
    incl = msk_ref[1] > 0.5
    eye = msk_ref[2]

    ch = RWKV_CHUNK
    n2 = 2 * ch
    m_c = lax.broadcasted_iota(jnp.int32, (ch, LANES), 1) < HEAD_DIM
    pairs = range(N_PAIRS)
    chunks = range(t // ch)
    chains = [(j, p) for j in chunks for p in pairs]
    rows = [slice(j * ch, (j + 1) * ch) for j in chunks]
    sls = [slice(p * LANES, (p + 1) * LANES) for p in pairs]
    blk = lambda arr, c: arr[rows[c[0]], sls[c[1]]]

    log_p = [_cumsum(log_w[rw], 0) for rw in rows]
    p_in = [jnp.exp(lp) for lp in log_p]
    p_inv = [jnp.exp(-lp) for lp in log_p]
    p_prev = [jnp.exp(lp - log_w[rw]) for lp, rw in zip(log_p, rows)]
    p_end = [jnp.exp(lp[ch - 1:ch] - lp) for lp in log_p]
    p_l = [jnp.exp(lp[ch - 1:ch]) for lp in log_p]
    dec = lambda arr, c: arr[c[0]][:, sls[c[1]]]

    kh = [blk(kappa, c) * lax.rsqrt(_pair_sum_lanes(blk(kap2, c), m_c) + 1e-12) for c in chains]
    b_v = [blk(a, c) * kh[i] for i, c in enumerate(chains)]
    stk = lambda val: _stack(val, m_c).astype(BF16)
    khs = [stk(kh[i] * dec(p_prev, c)) for i, c in enumerate(chains)]
    rs = [stk(blk(r, c) * dec(p_in, c)) for c in chains]
    bs = [stk(b_v[i] * dec(p_inv, c)) for i, c in enumerate(chains)]
    kts = [stk(blk(k_til, c) * dec(p_inv, c)) for c in chains]
    vs = [stk(blk(v, c)) for c in chains]
    kte = [stk(blk(k_til, c) * dec(p_end, c)) for c in chains]
    be = [stk(b_v[i] * dec(p_end, c)) for i, c in enumerate(chains)]
    idx = range(len(chains))

    g = [_dot_nt(jnp.concatenate([khs[i], rs[i]], axis=0), jnp.concatenate([bs[i], kts[i]], axis=0))
         for i in idx]
    a_ub = [jnp.where(strict, g[i][0:n2, 0:n2], 0.0) for i in idx]
    a_k = [jnp.concatenate([jnp.where(strict, g[i][0:n2, n2:], 0.0),
                            jnp.where(incl, g[i][n2:, n2:], 0.0)], axis=0).astype(BF16) for i in idx]
    a_rb = [jnp.where(incl, g[i][n2:, 0:n2], 0.0).astype(BF16) for i in idx]
    av = [_dot(a_k[i], vs[i]) for i in idx]

    xinv = [eye - a_ub[i] * msk_ref[3] for i in idx]
    for lvl in range(4, 9):
        xa = [_dot(xinv[i], a_ub[i] * msk_ref[lvl]) for i in idx]
        xinv = [xinv[i] - _dot(xa[i], xinv[i]) for i in idx]
    wu = [_dot(xinv[i], jnp.concatenate([khs[i], av[i][0:n2].astype(BF16)], axis=1)) for i in idx]

    bonus_sum = [_pair_sum_lanes(blk(rkk, c), m_c) for c in chains]
    s_st = [s_ref[p] for p in pairs]
    ys = []
    for j in chunks:
        ids = [j * N_PAIRS + p for p in pairs]
        ws = [_dot_nt(jnp.concatenate([wu[i][:, 0:LANES].astype(BF16), rs[i]], axis=0), s_st[p])
              for p, i in enumerate(ids)]
        ub = [(-(ws[p][0:n2] + wu[i][:, LANES:])).astype(BF16) for p, i in enumerate(ids)]
        ys += [ws[p][n2:] + av[i][n2:] + _dot(a_rb[i], ub[p]) for p, i in enumerate(ids)]
        s_st = [s_st[p] * p_l[j][:, sls[p]] + _dot_tn(jnp.concatenate([vs[i], ub[p]], axis=0),
                                                      jnp.concatenate([kte[i], be[i]], axis=0))
                for p, i in enumerate(ids)]
    for p in pairs:
        s_ref[p] = s_st[p]

    wkv = [ys[i][0:ch] + ys[i][ch:n2] for i in idx]
    mu = [_pair_sum_lanes(wkv[i], m_c) * (1.0 / HEAD_DIM) for i in idx]
    xc = [wkv[i] - mu[i] for i in idx]
    var = [_pair_sum_lanes(xc[i] * xc[i], m_c) * (1.0 / HEAD_DIM) for i in idx]
    for i, (j, p) in enumerate(chains):
        z = p_ref[0, rows[j], sh_w + p * LANES:sh_w + (p + 1) * LANES].astype(F32)
        y = (xc[i] * lax.rsqrt(var[i] + GN_EPS)) * gw_ref[:, sls[p]] + gb_ref[:, sls[p]] \
            + bonus_sum[i] * blk(v, chains[i])
        o_ref[0, rows[j], sls[p]] = (y * _silu(z)).astype(o_ref.dtype)


def _rwkv_masks():
    n = 2 * RWKV_CHUNK
    r = np.arange(n)[:, None]
    c = np.arange(n)[None, :]
    ms = [r > c, r >= c, r == c]
    b = 1
    while b < RWKV_CHUNK:
        ms.append((r // (2 * b) == c // (2 * b)) & (r % (2 * b) >= b) & (c % (2 * b) < b))
        b *= 2
    return jnp.asarray(np.stack(ms).astype(np.float32))


def _rwkv(p_b, mu, w_up, w0, a_up, a0, k_k, k_a, r_k, gn_w, gn_b):
    b, s, width = p_b.shape
    t = T_RWKV
    sh_w = 3 * G + 2 * LORA
    w_lo = jnp.zeros((LANES, 2 * G), F32).at[0:LORA, 0:G].set(w_up).at[LORA:, G:].set(a_up).astype(BF16)
    b0 = jnp.concatenate([w0, a0]).reshape(1, 2 * G)
    row = lambda a: a.reshape(1, -1)
    const = lambda shape: pl.BlockSpec(shape, lambda bi, ci: (0,) * len(shape))
    return pl.pallas_call(
        _rwkv_kernel, grid=(b, s // t),
        in_specs=[pl.BlockSpec((1, t, width), lambda bi, ci: (bi, ci, 0)),
                  const((1, sh_w)), const((LANES, 2 * G)), const((1, 2 * G)),
                  const((1, G)), const((1, G)), const((1, G)), const((1, G)), const((1, G)),
                  const((9, 2 * RWKV_CHUNK, 2 * RWKV_CHUNK))],
        out_specs=pl.BlockSpec((1, t, G), lambda bi, ci: (bi, ci, 0)),
        out_shape=jax.ShapeDtypeStruct((b, s, G), BF16),
        scratch_shapes=[pltpu.VMEM((SUBLANES, sh_w), F32),
                        pltpu.VMEM((N_PAIRS, LANES, LANES), F32)],
        compiler_params=pltpu.CompilerParams(dimension_semantics=("parallel", "arbitrary"),
                                             vmem_limit_bytes=VMEM_LIMIT),
        name="rwkv7",
    )(p_b, row(mu), w_lo, b0, row(k_k), row(k_a), row(r_k), row(gn_w), row(gn_b), _rwkv_masks())


def _lru_kernel(p_ref, cw_ref, cb_ref, wr_ref, br_ref, wi_ref, bi_ref, lam_ref, o_ref,
                prev_ref, h_ref):
    t = p_ref.shape[1]

    @pl.when(pl.program_id(1) == 0)
    def _():
        prev_ref[...] = jnp.zeros_like(prev_ref)
        h_ref[...] = jnp.zeros_like(h_ref)

    x = p_ref[0, :, 0:G].astype(F32)
    xc = _causal_conv(x, prev_ref[...], cw_ref, cb_ref)
    prev_ref[...] = x[t - SUBLANES:t, :]

    xb = xc.astype(BF16)
    r = _sigmoid(jnp.dot(xb, wr_ref[...], preferred_element_type=F32) + br_ref[...])
    i = _sigmoid(jnp.dot(xb, wi_ref[...], preferred_element_type=F32) + bi_ref[...])
    log_a = (-LRU_C * r) * _softplus(-lam_ref[...])
    a = jnp.exp(log_a)
    th = jnp.tanh(log_a)
    u = jnp.sqrt(-2.0 * th / (1.0 - th)) * (i * xc)

    sub = lax.broadcasted_iota(jnp.int32, (SUBLANES, G), 0)
    carry = h_ref[...]
    groups = []
    for i in range(t // SUBLANES):
        rows = slice(i * SUBLANES, (i + 1) * SUBLANES)
        ug, ag = u[rows], a[rows]
        d = 1
        while d < SUBLANES:
            keep = sub >= d
            ug = ug + ag * jnp.where(keep, pltpu.roll(ug, d, axis=0), 0.0)
            ag = ag * jnp.where(keep, pltpu.roll(ag, d, axis=0), 1.0)
            d *= 2
        hg = ug + ag * carry
        carry = hg[SUBLANES - 1:SUBLANES]
        groups.append(hg)
    h_ref[...] = carry
    h = jnp.concatenate(groups, axis=0)
    o_ref[0] = (h * _silu(p_ref[0, :, G:2 * G].astype(F32))).astype(o_ref.dtype)


def _block_diag_weight(w):
    eye = jnp.eye(N_HEADS, dtype=w.dtype)
    return jnp.einsum("hij,hg->higj", w, eye).reshape(G, G)


def _lru(p_c, conv_w, conv_b, w_r, b_r, w_i, b_i, lam):
    b, s, width = p_c.shape
    t = T_LRU
    row = lambda a: a.reshape(1, -1)
    const = lambda shape: pl.BlockSpec(shape, lambda bi, ci: (0,) * len(shape))
    return pl.pallas_call(
        _lru_kernel, grid=(b, s // t),
        in_specs=[pl.BlockSpec((1, t, width), lambda bi, ci: (bi, ci, 0)),
                  const((CONV_W, G)), const((1, G)), const((G, G)), const((1, G)),
                  const((G, G)), const((1, G)), const((1, G))],
        out_specs=pl.BlockSpec((1, t, G), lambda bi, ci: (bi, ci, 0)),
        out_shape=jax.ShapeDtypeStruct((b, s, G), BF16),
        scratch_shapes=[pltpu.VMEM((SUBLANES, G), F32), pltpu.VMEM((1, G), F32)],
        compiler_params=pltpu.CompilerParams(dimension_semantics=("parallel", "arbitrary"),
                                             vmem_limit_bytes=VMEM_LIMIT),
        name="rglru",
    )(p_c, conv_w, row(conv_b), _block_diag_weight(w_r).astype(BF16), row(b_r),
      _block_diag_weight(w_i).astype(BF16), row(b_i), row(lam))


def _ret_kernel(p_ref, cos_ref, sin_ref, dm_ref, xi_ref, zeta_ref, g_ref, nw_ref, o_ref, r_ref):
    t = p_ref.shape[1]

    @pl.when(pl.program_id(1) == 0)
    def _():
        r_ref[...] = jnp.zeros_like(r_ref)

    lane = lax.broadcasted_iota(jnp.int32, (t, LANES), 1)
    m_e = lane < HEAD_DIM
    first_half = (lane % HEAD_DIM) < (HEAD_DIM // 2)
    bd = _block_diag_mask()
    ones2 = _head_ones()
    cos = cos_ref[...]
    sin = sin_ref[...]

    def rope(x):
        swapped = jnp.where(first_half, pltpu.roll(x, LANES - HEAD_DIM // 2, axis=1),
                            pltpu.roll(x, HEAD_DIM // 2, axis=1))
        return x * cos + swapped * sin

    pairs = range(N_PAIRS)
    heads = range(N_HEADS)
    sls = [slice(p * LANES, (p + 1) * LANES) for p in pairs]
    q = [rope(p_ref[0, :, sl].astype(F32)) for sl in sls]
    k = [rope(p_ref[0, :, G + p * LANES:G + (p + 1) * LANES].astype(F32)) * (HEAD_DIM ** -0.5)
         for p in pairs]
    v = [p_ref[0, :, 2 * G + p * LANES:2 * G + (p + 1) * LANES] for p in pairs]
    kb = [x.astype(BF16) for x in k]
    vb = [x.astype(BF16) for x in v]
    r_st = [r_ref[p] for p in pairs]
    qm = [(jnp.where(m_e, q[h // 2], 0.0) if h % 2 == 0 else jnp.where(m_e, 0.0, q[h // 2])).astype(BF16)
          for h in heads]
    s = [_dot_nt(qm[h], kb[h // 2]) * dm_ref[h] for h in heads]
    q_r = [_dot(q[p], r_st[p]) for p in pairs]
    o_h = [_dot(s[h], vb[h // 2]) for h in heads]
    kv = [_dot_tn(k[p] * zeta_ref[p], v[p]) for p in pairs]
    for p, sl in enumerate(sls):
        r_ref[p] = g_ref[p] * r_st[p] + jnp.where(bd, kv[p], 0.0)
        o = jnp.where(m_e, o_h[2 * p], o_h[2 * p + 1]) + xi_ref[p] * q_r[p]
        z = p_ref[0, :, 3 * G + p * LANES:3 * G + (p + 1) * LANES].astype(F32)
        o_ref[0, :, sl] = (_pair_layer_norm(o, ones2) * nw_ref[:, sl] * _silu(z)).astype(o_ref.dtype)


def _ret_tables(s):
    t = T_RET
    half = HEAD_DIM // 2
    pos = jnp.arange(s, dtype=F32)
    inv_freq = ROPE_THETA ** (-jnp.arange(half, dtype=F32) / half)
    ang = pos[:, None] * inv_freq[None, :]
    cos = jnp.tile(jnp.cos(ang), (1, LANES // half))
    sin = jnp.sin(ang)
    sin = jnp.tile(jnp.concatenate([-sin, sin], axis=-1), (1, LANES // HEAD_DIM))
    log_g = jnp.log1p(-jnp.exp2(-5.0 - jnp.arange(N_HEADS, dtype=F32)))
    idx = jnp.arange(t, dtype=F32)
    chunk = jnp.arange(t) // RET_CHUNK
    visible = chunk[:, None] >= chunk[None, :]
    dm = jnp.where(visible[None], jnp.exp(log_g[:, None, None] * jnp.abs(idx[:, None] - idx[None, :])), 0.0)
    pair_lanes = lambda a: jnp.repeat(a.reshape(N_PAIRS, 2, -1), HEAD_DIM, axis=1).transpose(0, 2, 1)
    xi = pair_lanes(jnp.exp(log_g[:, None] * (idx + 1.0)))
    zeta = pair_lanes(jnp.exp(log_g[:, None] * (t - 1.0 - idx)))
    g_blk = pair_lanes(jnp.exp(log_g * t)[:, None])
    return cos, sin, dm, xi, zeta, g_blk


def _retention(p_d, norm_w, tables):
    b, s, width = p_d.shape
    t = T_RET
    cos, sin, dm, xi, zeta, g_blk = tables
    const = lambda shape: pl.BlockSpec(shape, lambda bi, ci: (0,) * len(shape))
    return pl.pallas_call(
        _ret_kernel, grid=(b, s // t),
        in_specs=[pl.BlockSpec((1, t, width), lambda bi, ci: (bi, ci, 0)),
                  pl.BlockSpec((t, LANES), lambda bi, ci: (ci, 0)),
                  pl.BlockSpec((t, LANES), lambda bi, ci: (ci, 0)),
                  const((N_HEADS, t, t)), const((N_PAIRS, t, LANES)), const((N_PAIRS, t, LANES)),
                  const((N_PAIRS, 1, LANES)), const((1, G))],
        out_specs=pl.BlockSpec((1, t, G), lambda bi, ci: (bi, ci, 0)),
        out_shape=jax.ShapeDtypeStruct((b, s, G), BF16),
        scratch_shapes=[pltpu.VMEM((N_PAIRS, LANES, LANES), F32)],
        compiler_params=pltpu.CompilerParams(dimension_semantics=("parallel", "arbitrary"),
                                             vmem_limit_bytes=VMEM_LIMIT),
        name="retention",
    )(p_d, cos, sin, dm, xi, zeta, g_blk, norm_w.reshape(1, -1))


def kernel(x, norm_pre, norm_post, w_in, w_out, mlstm_conv_w, mlstm_conv_b, mlstm_i_bias, mlstm_f_bias, mlstm_norm_w, rwkv_mu, rwkv_w_up, rwkv_w0, rwkv_a_up, rwkv_a0, rwkv_k_k, rwkv_k_a, rwkv_r_k, rwkv_gn_w, rwkv_gn_b, lru_conv_w, lru_conv_b, lru_w_r, lru_b_r, lru_w_i, lru_b_i, lru_lambda, ret_norm_w):
    b, s, d = x.shape
    depth = w_in.shape[0]
    tables = _ret_tables(s)
    x2 = x.reshape(b * s, d)
    for l in range(depth):
        w = w_in[l]
        gain = norm_pre[l].reshape(1, d)
        w_a = jnp.concatenate([w[:, _A0:_B0], jnp.zeros((d, LANES - 2 * N_HEADS), F32)],
                              axis=1).astype(BF16)
        p_a, gates = _proj(x2, gain, w_a, [(_AG, BF16), (LANES, F32)], "in_proj_mlstm")
        p_b, = _proj(x2, gain, w[:, _B0:_C0].astype(BF16), [(_C0 - _B0, BF16)], "in_proj_rwkv7")
        p_c, = _proj(x2, gain, w[:, _C0:_D0].astype(BF16), [(_D0 - _C0, BF16)], "in_proj_rglru")
        p_d, = _proj(x2, gain, w[:, _D0:_END].astype(BF16), [(_END - _D0, BF16)], "in_proj_retention")

        y_a = _mlstm(p_a.reshape(b, s, -1), gates.reshape(b, s, -1), mlstm_conv_w[l], mlstm_conv_b[l],
                     mlstm_i_bias[l], mlstm_f_bias[l], mlstm_norm_w[l])
        y_b = _rwkv(p_b.reshape(b, s, -1), rwkv_mu[l], rwkv_w_up[l], rwkv_w0[l], rwkv_a_up[l],
                    rwkv_a0[l], rwkv_k_k[l], rwkv_k_a[l], rwkv_r_k[l], rwkv_gn_w[l], rwkv_gn_b[l])
        y_c = _lru(p_c.reshape(b, s, -1), lru_conv_w[l], lru_conv_b[l], lru_w_r[l], lru_b_r[l],
                   lru_w_i[l], lru_b_i[l], lru_lambda[l])
        y_d = _retention(p_d.reshape(b, s, -1), ret_norm_w[l], tables)

        ys = [y.reshape(b * s, G) for y in (y_a, y_b, y_c, y_d)]
        x2 = _out_proj(ys, w_out[l].reshape(4, G, d).astype(BF16), norm_post[l].reshape(1, d), x2)
    return x2.reshape(b, s, d)
```

```python
import functools

import numpy as np
import jax
import jax.numpy as jnp
from jax import lax
from jax.experimental import pallas as pl
from jax.experimental.pallas import tpu as pltpu

F32 = jnp.float32
BF16 = jnp.bfloat16

D_MODEL = 1024
G = 512
N_HEADS = 8
HEAD_DIM = 64
N_PAIRS = N_HEADS // 2
LANES = 128
SUBLANES = 8
CONV_W = 4
LORA = 64
LRU_C = 8.0
W_DECAY_SCALE = 0.606531
ROPE_THETA = 10000.0
NORM_EPS = 1e-6
GN_EPS = 1e-5
RET_CHUNK = 64
NEG = -1e30

_A0, _AG, _B0, _C0, _D0, _END = 0, 2560, 2576, 4752, 5776, 7824

T_MLSTM = 128
T_RET = 256
RET_SB = 128
RWKV_CHUNK = 64
T_RWKV = 256
T_LRU = 512
TM_PROJ = 512
VMEM_LIMIT = 48 * 1024 * 1024


def _sigmoid(x):
    return jax.nn.sigmoid(x)


def _silu(x):
    return x * jax.nn.sigmoid(x)


def _softplus(x):
    return jnp.maximum(x, 0.0) + jnp.log1p(jnp.exp(-jnp.abs(x)))


def _log_sigmoid(x):
    return -_softplus(-x)


def _dot(a, b):
    return jnp.dot(a.astype(BF16), b.astype(BF16), preferred_element_type=F32)


def _dot_nt(a, b):
    return lax.dot_general(a.astype(BF16), b.astype(BF16), (((1,), (1,)), ((), ())),
                           preferred_element_type=F32)


def _dot_tn(a, b):
    return lax.dot_general(a.astype(BF16), b.astype(BF16), (((0,), (0,)), ((), ())),
                           preferred_element_type=F32)


def _shift_rows(x, prev8, j):
    xr = pltpu.roll(x, j, axis=0)
    pr = pltpu.roll(prev8, j, axis=0)
    row = lax.broadcasted_iota(jnp.int32, prev8.shape, 0)
    first = jnp.where(row < j, pr, xr[0:SUBLANES])
    return jnp.concatenate([first, xr[SUBLANES:]], axis=0)


def _cumsum(x, axis):
    n = x.shape[axis]
    idx = lax.broadcasted_iota(jnp.int32, x.shape, axis)
    d = 1
    while d < n:
        x = x + jnp.where(idx >= d, pltpu.roll(x, d, axis=axis), 0.0)
        d *= 2
    return x


def _causal_conv(xb, prev8, sh_ref, w_ref, b_ref):
    row8 = lax.broadcasted_iota(jnp.int32, prev8.shape, 0)
    y = xb.astype(F32) * w_ref[CONV_W - 1:CONV_W, :] + b_ref[...]
    for j in range(1, CONV_W):
        xs = jnp.dot(sh_ref[j - 1], xb, preferred_element_type=F32)
        head = xs[0:SUBLANES] + jnp.where(row8 < j, pltpu.roll(prev8, j, axis=0), 0.0)
        xs = jnp.concatenate([head, xs[SUBLANES:]], axis=0)
        y = y + xs * w_ref[CONV_W - 1 - j:CONV_W - j, :]
    return y


def _causal_conv_rolls(x, prev8, w_ref, b_ref):
    y = x * w_ref[CONV_W - 1:CONV_W, :] + b_ref[...]
    for j in range(1, CONV_W):
        y = y + _shift_rows(x, prev8, j) * w_ref[CONV_W - 1 - j:CONV_W - j, :]
    return y


def _shift_matrices(t):
    r = np.arange(t)[:, None]
    c = np.arange(t)[None, :]
    return jnp.asarray(np.stack([(r - c == j) for j in range(1, CONV_W)]).astype(np.float32)).astype(BF16)


def _split2(x):
    hi = x.astype(BF16)
    lo = (x - hi.astype(F32)).astype(BF16)
    return jnp.concatenate([hi, lo], axis=1)


def _head_ones():
    r = lax.broadcasted_iota(jnp.int32, (2 * LANES, LANES), 0) % LANES
    c = lax.broadcasted_iota(jnp.int32, (2 * LANES, LANES), 1)
    return ((r < HEAD_DIM) == (c < HEAD_DIM)).astype(BF16)


def _pair_sum(x, ones2):
    return jnp.dot(_split2(x), ones2, preferred_element_type=F32)


def _pair_sum_lanes(x, m_e):
    se = jnp.sum(jnp.where(m_e, x, 0.0), axis=-1, keepdims=True)
    so = jnp.sum(jnp.where(m_e, 0.0, x), axis=-1, keepdims=True)
    return jnp.where(m_e, se, so)


def _pair_layer_norm(x, ones2):
    mu = _pair_sum(x, ones2) * (1.0 / HEAD_DIM)
    xc = x - mu
    var = _pair_sum(xc * xc, ones2) * (1.0 / HEAD_DIM)
    return xc * lax.rsqrt(var + GN_EPS)


def _block_diag_mask():
    r = lax.broadcasted_iota(jnp.int32, (LANES, LANES), 0)
    c = lax.broadcasted_iota(jnp.int32, (LANES, LANES), 1)
    return (r < HEAD_DIM) == (c < HEAD_DIM)


def _proj_kernel(x_ref, g_ref, w_ref, *o_refs):
    x = x_ref[...]
    rs = lax.rsqrt(jnp.mean(x * x, axis=-1, keepdims=True) + NORM_EPS)
    acc = jnp.dot((x * g_ref[...]).astype(BF16), w_ref[...], preferred_element_type=F32) * rs
    off = 0
    for o_ref in o_refs:
        n = o_ref.shape[1]
        o_ref[...] = acc[:, off:off + n].astype(o_ref.dtype)
        off += n


def _proj(x2, gain, w, outs, name):
    m, d = x2.shape
    n = w.shape[1]
    tm = TM_PROJ
    res = pl.pallas_call(
        _proj_kernel, grid=(m // tm,),
        in_specs=[pl.BlockSpec((tm, d), lambda i: (i, 0)),
                  pl.BlockSpec((1, d), lambda i: (0, 0)),
                  pl.BlockSpec((d, n), lambda i: (0, 0))],
        out_specs=[pl.BlockSpec((tm, wd), lambda i: (i, 0)) for wd, _ in outs],
        out_shape=[jax.ShapeDtypeStruct((m, wd), dt) for wd, dt in outs],
        compiler_params=pltpu.CompilerParams(dimension_semantics=("parallel",),
                                             vmem_limit_bytes=VMEM_LIMIT),
        name=name,
    )(x2, gain, w)
    return res


def _out_kernel(ya_ref, yb_ref, yc_ref, yd_ref, w_ref, g_ref, x_ref, o_ref):
    acc = jnp.dot(ya_ref[...].astype(BF16), w_ref[0], preferred_element_type=F32)
    acc += jnp.dot(yb_ref[...].astype(BF16), w_ref[1], preferred_element_type=F32)
    acc += jnp.dot(yc_ref[...].astype(BF16), w_ref[2], preferred_element_type=F32)
    acc += jnp.dot(yd_ref[...].astype(BF16), w_ref[3], preferred_element_type=F32)
    ms = jnp.mean(acc * acc, axis=-1, keepdims=True)
    o_ref[...] = x_ref[...] + (acc * lax.rsqrt(ms + NORM_EPS)) * g_ref[...]


def _out_proj(ys, w4, gain, x2):
    m, d = x2.shape
    tm = TM_PROJ
    yspec = pl.BlockSpec((tm, G), lambda i: (i, 0))
    return pl.pallas_call(
        _out_kernel, grid=(m // tm,),
        in_specs=[yspec, yspec, yspec, yspec,
                  pl.BlockSpec((4, G, d), lambda i: (0, 0, 0)),
                  pl.BlockSpec((1, d), lambda i: (0, 0)),
                  pl.BlockSpec((tm, d), lambda i: (i, 0))],
        out_specs=pl.BlockSpec((tm, d), lambda i: (i, 0)),
        out_shape=jax.ShapeDtypeStruct((m, d), F32),
        compiler_params=pltpu.CompilerParams(dimension_semantics=("parallel",),
                                             vmem_limit_bytes=VMEM_LIMIT),
        name="out_proj",
    )(*ys, w4, gain, x2)


def _mlstm_kernel(p_ref, g_ref, sh_ref, cw_ref, cb_ref, bcol_ref, nw_ref, o_ref,
                  prev_ref, c_ref, n_ref, m_ref):
    t = p_ref.shape[1]

    @pl.when(pl.program_id(1) == 0)
    def _():
        prev_ref[...] = jnp.zeros_like(prev_ref)
        c_ref[...] = jnp.zeros_like(c_ref)
        n_ref[...] = jnp.zeros_like(n_ref)
        m_ref[...] = jnp.zeros_like(m_ref)

    x_qk = p_ref[0, :, 0:2 * G]
    qk = _silu(_causal_conv(x_qk, prev_ref[...], sh_ref, cw_ref, cb_ref))
    prev_ref[...] = p_ref[0, t - 2 * SUBLANES:t, 0:2 * G].astype(F32)[SUBLANES:]

    lane = lax.broadcasted_iota(jnp.int32, (t, LANES), 1)
    m_e = lane < HEAD_DIM
    ones2 = _head_ones()
    ones_row = jnp.ones((2 * t, LANES), BF16)

    gcol = g_ref[0] + bcol_ref[...]
    cumc = _cumsum(_log_sigmoid(gcol), 0)
    grow = jnp.where(lane < N_HEADS, gcol, cumc).T
    i_row = grow[0:N_HEADS]
    cumr = grow[N_HEADS:2 * N_HEADS]

    m_e1 = lax.broadcasted_iota(jnp.int32, (1, LANES), 1) < HEAD_DIM
    tril = (lax.broadcasted_iota(jnp.int32, (t, t), 0) >= lax.broadcasted_iota(jnp.int32, (t, t), 1))
    bd = _block_diag_mask()

    pairs = range(N_PAIRS)
    heads = range(N_HEADS)
    sls = [slice(p * LANES, (p + 1) * LANES) for p in pairs]
    sel = lambda xs, p: jnp.where(m_e, xs[2 * p], xs[2 * p + 1])
    sel1 = lambda xs, p: jnp.where(m_e1, xs[2 * p], xs[2 * p + 1])

    q = [qk[:, sl] for sl in sls]
    k = [qk[:, G + p * LANES:G + (p + 1) * LANES] * (HEAD_DIM ** -0.5) for p in pairs]
    v = [p_ref[0, :, 2 * G + p * LANES:2 * G + (p + 1) * LANES] for p in pairs]
    kb = [x.astype(BF16) for x in k]
    vb = [x.astype(BF16) for x in v]
    c_st = [c_ref[p] for p in pairs]
    n_st = [n_ref[p] for p in pairs]
    m_st = [m_ref[p] for p in pairs]

    qm = [(jnp.where(m_e, q[h // 2], 0.0) if h % 2 == 0 else jnp.where(m_e, 0.0, q[h // 2])).astype(BF16)
          for h in heads]
    s_raw = [_dot_nt(qm[h], kb[h // 2]) for h in heads]
    q_c = [_dot_nt(q[p], c_st[p]) for p in pairs]

    cum_c = [cumc[:, N_HEADS + h:N_HEADS + h + 1] for h in heads]
    i_c = [gcol[:, h:h + 1] for h in heads]
    m_prev = [m_st[h // 2][:, (h % 2) * HEAD_DIM:(h % 2) * HEAD_DIM + 1] for h in heads]
    log_d = [jnp.where(tril, cum_c[h] - cumr[h:h + 1, :] + i_row[h:h + 1, :], NEG) for h in heads]
    inter = [cum_c[h] + m_prev[h] for h in heads]
    m_t = [jnp.maximum(inter[h], jnp.max(log_d[h], axis=-1, keepdims=True)) for h in heads]
    s = [s_raw[h] * jnp.exp(log_d[h] - m_t[h]) for h in heads]
    s2 = [_split2(s[h]) for h in heads]
    num_h = [jnp.dot(s2[h][:, 0:t], vb[h // 2], preferred_element_type=F32) for h in heads]
    den_h = [jnp.dot(s2[h], ones_row, preferred_element_type=F32) for h in heads]
    s_int_h = [jnp.exp(inter[h] - m_t[h]) for h in heads]

    tot = [cum_c[h][t - 1:t, :] for h in heads]
    log_w = [tot[h] - cum_c[h] + i_c[h] for h in heads]
    m_new = [jnp.maximum(tot[h] + m_prev[h], jnp.max(log_w[h], axis=0, keepdims=True)) for h in heads]
    wj_h = [jnp.exp(log_w[h] - m_new[h]) for h in heads]
    sc_h = [jnp.exp(tot[h] + m_prev[h] - m_new[h]) for h in heads]

    kw = [k[p] * sel(wj_h, p) for p in pairs]
    vk = [_dot_tn(v[p], kw[p]) for p in pairs]
    for p in pairs:
        sc = sel1(sc_h, p)
        c_ref[p] = sc * c_st[p] + jnp.where(bd, vk[p], 0.0)
        n_ref[p] = sc * n_st[p] + jnp.sum(kw[p], axis=0, keepdims=True)
        m_ref[p] = sel1(m_new, p)

    for p, sl in enumerate(sls):
        s_int = sel(s_int_h, p)
        num = sel(num_h, p) + s_int * q_c[p]
        den = sel(den_h, p) + s_int * _pair_sum(q[p] * n_st[p], ones2)
        h_out = num / jnp.maximum(jnp.abs(den), jnp.exp(-sel(m_t, p)))
        o_gate = _sigmoid(p_ref[0, :, 3 * G + p * LANES:3 * G + (p + 1) * LANES].astype(F32))
        z = p_ref[0, :, 4 * G + p * LANES:4 * G + (p + 1) * LANES].astype(F32)
        y = _pair_layer_norm(h_out * o_gate, ones2) * nw_ref[:, sl]
        o_ref[0, :, sl] = (y * _silu(z)).astype(o_ref.dtype)


def _mlstm(p_a, gates, conv_w, conv_b, i_bias, f_bias, norm_w):
    b, s, width = p_a.shape
    t = T_MLSTM
    nc = s // t
    bcol = jnp.zeros((1, LANES), F32).at[0, 0:N_HEADS].set(i_bias).at[0, N_HEADS:2 * N_HEADS].set(f_bias)
    const = lambda shape: pl.BlockSpec(shape, lambda bi, ci: (0,) * len(shape))
    return pl.pallas_call(
        _mlstm_kernel, grid=(b, nc),
        in_specs=[pl.BlockSpec((1, t, width), lambda bi, ci: (bi, ci, 0)),
                  pl.BlockSpec((1, t, LANES), lambda bi, ci: (bi, ci, 0)),
                  const((CONV_W - 1, t, t)),
                  const((CONV_W, 2 * G)), const((1, 2 * G)), const((1, LANES)), const((1, G))],
        out_specs=pl.BlockSpec((1, t, G), lambda bi, ci: (bi, ci, 0)),
        out_shape=jax.ShapeDtypeStruct((b, s, G), BF16),
        scratch_shapes=[pltpu.VMEM((SUBLANES, 2 * G), F32),
                        pltpu.VMEM((N_PAIRS, LANES, LANES), F32),
                        pltpu.VMEM((N_PAIRS, 1, LANES), F32),
                        pltpu.VMEM((N_PAIRS, 1, LANES), F32)],
        compiler_params=pltpu.CompilerParams(dimension_semantics=("parallel", "arbitrary"),
                                             vmem_limit_bytes=VMEM_LIMIT),
        name="mlstm",
    )(p_a, gates, _shift_matrices(t), conv_w, conv_b.reshape(1, -1), bcol, norm_w.reshape(1, -1))


def _stack(x, m_e):
    return jnp.concatenate([jnp.where(m_e, x, 0.0), jnp.where(m_e, 0.0, x)], axis=0)


def _rwkv_kernel(p_ref, mu_ref, wlo_ref, b0_ref, kk_ref, ka_ref, rk_ref, gw_ref, gb_ref,
                 msk_ref, o_ref, prev_ref, s_ref):
    t = p_ref.shape[1]
    sh_w = 3 * G + 2 * LORA

    @pl.when(pl.program_id(1) == 0)
    def _():
        prev_ref[...] = jnp.zeros_like(prev_ref)
        s_ref[...] = jnp.zeros_like(s_ref)

    x = p_ref[0, :, 0:sh_w].astype(F32)
    xs = x + mu_ref[...] * (_shift_rows(x, prev_ref[...], 1) - x)
    prev_ref[...] = x[t - SUBLANES:t, :]

    r = xs[:, 0:G]
    k = xs[:, G:2 * G]
    v = xs[:, 2 * G:3 * G]
    lo = xs[:, 3 * G:sh_w]
    lane = lax.broadcasted_iota(jnp.int32, (t, LANES), 1)
    m_e = lane < HEAD_DIM
    lo = jnp.where(m_e, jnp.tanh(lo), lo)
    pre = _dot(lo, wlo_ref[...]) + b0_ref[...]
    log_w = -W_DECAY_SCALE * _sigmoid(pre[:, 0:G])
    a = _sigmoid(pre[:, G:2 * G])

    kappa = k * kk_ref[...]
    k_til = k * (1.0 + (a - 1.0) * ka_ref[...])
    rkk = r * rk_ref[...] * k_til
    kap2 = kappa * kappa

    strict = msk_ref[0] > 0.5
    incl = msk_ref[1] > 0.5
    eye = msk_ref[2]

    ch = RWKV_CHUNK
    n2 = 2 * ch
    m_c = lax.broadcasted_iota(jnp.int32, (ch, LANES), 1) < HEAD_DIM
    pairs = range(N_PAIRS)
    chunks = range(t // ch)
    chains = [(j, p) for j in chunks for p in pairs]
    rows = [slice(j * ch, (j + 1) * ch) for j in chunks]
    sls = [slice(p * LANES, (p + 1) * LANES) for p in pairs]
    blk = lambda arr, c: arr[rows[c[0]], sls[c[1]]]

    log_p = [_cumsum(log_w[rw], 0) for rw in rows]
    p_in = [jnp.exp(lp) for lp in log_p]
    p_inv = [jnp.exp(-lp) for lp in log_p]
    p_prev = [jnp.exp(lp - log_w[rw]) for lp, rw in zip(log_p, rows)]
    p_end = [jnp.exp(lp[ch - 1:ch] - lp) for lp in log_p]
    p_l = [jnp.exp(lp[ch - 1:ch]) for lp in log_p]
    dec = lambda arr, c: arr[c[0]][:, sls[c[1]]]

    kh = [blk(kappa, c) * lax.rsqrt(_pair_sum_lanes(blk(kap2, c), m_c) + 1e-12) for c in chains]
    b_v = [blk(a, c) * kh[i] for i, c in enumerate(chains)]
    stk = lambda val: _stack(val, m_c).astype(BF16)
    khs = [stk(kh[i] * dec(p_prev, c)) for i, c in enumerate(chains)]
    rs = [stk(blk(r, c) * dec(p_in, c)) for c in chains]
    bs = [stk(b_v[i] * dec(p_inv, c)) for i, c in enumerate(chains)]
    kts = [stk(blk(k_til, c) * dec(p_inv, c)) for c in chains]
    vs = [stk(blk(v, c)) for c in chains]
    kte = [stk(blk(k_til, c) * dec(p_end, c)) for c in chains]
    be = [stk(b_v[i] * dec(p_end, c)) for i, c in enumerate(chains)]
    idx = range(len(chains))

    g = [_dot_nt(jnp.concatenate([khs[i], rs[i]], axis=0), jnp.concatenate([bs[i], kts[i]], axis=0))
         for i in idx]
    a_ub = [jnp.where(strict, g[i][0:n2, 0:n2], 0.0) for i in idx]
    a_k = [jnp.concatenate([jnp.where(strict, g[i][0:n2, n2:], 0.0),
                            jnp.where(incl, g[i][n2:, n2:], 0.0)], axis=0).astype(BF16) for i in idx]
    a_rb = [jnp.where(incl, g[i][n2:, 0:n2], 0.0).astype(BF16) for i in idx]
    av = [_dot(a_k[i], vs[i]) for i in idx]

    xinv = [eye - a_ub[i] * msk_ref[3] for i in idx]
    for lvl in range(4, 9):
        xa = [_dot(xinv[i], a_ub[i] * msk_ref[lvl]) for i in idx]
        xinv = [xinv[i] - _dot(xa[i], xinv[i]) for i in idx]
    wu = [_dot(xinv[i], jnp.concatenate([khs[i], av[i][0:n2].astype(BF16)], axis=1)) for i in idx]

    bonus_sum = [_pair_sum_lanes(blk(rkk, c), m_c) for c in chains]
    s_st = [s_ref[p] for p in pairs]
    ys = []
    for j in chunks:
        ids = [j * N_PAIRS + p for p in pairs]
        ws = [_dot_nt(jnp.concatenate([wu[i][:, 0:LANES].astype(BF16), rs[i]], axis=0), s_st[p])
              for p, i in enumerate(ids)]
        ub = [(-(ws[p][0:n2] + wu[i][:, LANES:])).astype(BF16) for p, i in enumerate(ids)]
        ys += [ws[p][n2:] + av[i][n2:] + _dot(a_rb[i], ub[p]) for p, i in enumerate(ids)]
        s_st = [s_st[p] * p_l[j][:, sls[p]] + _dot_tn(jnp.concatenate([vs[i], ub[p]], axis=0),
                                                      jnp.concatenate([kte[i], be[i]], axis=0))
                for p, i in enumerate(ids)]
    for p in pairs:
        s_ref[p] = s_st[p]

    wkv = [ys[i][0:ch] + ys[i][ch:n2] for i in idx]
    mu = [_pair_sum_lanes(wkv[i], m_c) * (1.0 / HEAD_DIM) for i in idx]
    xc = [wkv[i] - mu[i] for i in idx]
    var = [_pair_sum_lanes(xc[i] * xc[i], m_c) * (1.0 / HEAD_DIM) for i in idx]
    for i, (j, p) in enumerate(chains):
        z = p_ref[0, rows[j], sh_w + p * LANES:sh_w + (p + 1) * LANES].astype(F32)
        y = (xc[i] * lax.rsqrt(var[i] + GN_EPS)) * gw_ref[:, sls[p]] + gb_ref[:, sls[p]] \
            + bonus_sum[i] * blk(v, chains[i])
        o_ref[0, rows[j], sls[p]] = (y * _silu(z)).astype(o_ref.dtype)


def _rwkv_masks():
    n = 2 * RWKV_CHUNK
    r = np.arange(n)[:, None]
    c = np.arange(n)[None, :]
    ms = [r > c, r >= c, r == c]
    b = 1
    while b < RWKV_CHUNK:
        ms.append((r // (2 * b) == c // (2 * b)) & (r % (2 * b) >= b) & (c % (2 * b) < b))
        b *= 2
    return jnp.asarray(np.stack(ms).astype(np.float32))


def _rwkv(p_b, mu, w_up, w0, a_up, a0, k_k, k_a, r_k, gn_w, gn_b):
    b, s, width = p_b.shape
    t = T_RWKV
    sh_w = 3 * G + 2 * LORA
    w_lo = jnp.zeros((LANES, 2 * G), F32).at[0:LORA, 0:G].set(w_up).at[LORA:, G:].set(a_up).astype(BF16)
    b0 = jnp.concatenate([w0, a0]).reshape(1, 2 * G)
    row = lambda a: a.reshape(1, -1)
    const = lambda shape: pl.BlockSpec(shape, lambda bi, ci: (0,) * len(shape))
    return pl.pallas_call(
        _rwkv_kernel, grid=(b, s // t),
        in_specs=[pl.BlockSpec((1, t, width), lambda bi, ci: (bi, ci, 0)),
                  const((1, sh_w)), const((LANES, 2 * G)), const((1, 2 * G)),
                  const((1, G)), const((1, G)), const((1, G)), const((1, G)), const((1, G)),
                  const((9, 2 * RWKV_CHUNK, 2 * RWKV_CHUNK))],
        out_specs=pl.BlockSpec((1, t, G), lambda bi, ci: (bi, ci, 0)),
        out_shape=jax.ShapeDtypeStruct((b, s, G), BF16),
        scratch_shapes=[pltpu.VMEM((SUBLANES, sh_w), F32),
                        pltpu.VMEM((N_PAIRS, LANES, LANES), F32)],
        compiler_params=pltpu.CompilerParams(dimension_semantics=("parallel", "arbitrary"),
                                             vmem_limit_bytes=VMEM_LIMIT),
        name="rwkv7",
    )(p_b, row(mu), w_lo, b0, row(k_k), row(k_a), row(r_k), row(gn_w), row(gn_b), _rwkv_masks())


def _lru_kernel(p_ref, cw_ref, cb_ref, wr_ref, br_ref, wi_ref, bi_ref, lam_ref, o_ref,
                prev_ref, h_ref):
    t = p_ref.shape[1]

    @pl.when(pl.program_id(1) == 0)
    def _():
        prev_ref[...] = jnp.zeros_like(prev_ref)
        h_ref[...] = jnp.zeros_like(h_ref)

    x = p_ref[0, :, 0:G].astype(F32)
    xc = _causal_conv_rolls(x, prev_ref[...], cw_ref, cb_ref)
    prev_ref[...] = x[t - SUBLANES:t, :]

    xb = xc.astype(BF16)
    r = _sigmoid(jnp.dot(xb, wr_ref[...], preferred_element_type=F32) + br_ref[...])
    i = _sigmoid(jnp.dot(xb, wi_ref[...], preferred_element_type=F32) + bi_ref[...])
    log_a = (-LRU_C * r) * _softplus(-lam_ref[...])
    a = jnp.exp(log_a)
    th = jnp.tanh(log_a)
    u = jnp.sqrt(-2.0 * th / (1.0 - th)) * (i * xc)

    sub = lax.broadcasted_iota(jnp.int32, (SUBLANES, G), 0)
    carry = h_ref[...]
    groups = []
    for i in range(t // SUBLANES):
        rows = slice(i * SUBLANES, (i + 1) * SUBLANES)
        ug, ag = u[rows], a[rows]
        d = 1
        while d < SUBLANES:
            keep = sub >= d
            ug = ug + ag * jnp.where(keep, pltpu.roll(ug, d, axis=0), 0.0)
            ag = ag * jnp.where(keep, pltpu.roll(ag, d, axis=0), 1.0)
            d *= 2
        hg = ug + ag * carry
        carry = hg[SUBLANES - 1:SUBLANES]
        groups.append(hg)
    h_ref[...] = carry
    h = jnp.concatenate(groups, axis=0)
    o_ref[0] = (h * _silu(p_ref[0, :, G:2 * G].astype(F32))).astype(o_ref.dtype)


def _block_diag_weight(w):
    eye = jnp.eye(N_HEADS, dtype=w.dtype)
    return jnp.einsum("hij,hg->higj", w, eye).reshape(G, G)


def _lru(p_c, conv_w, conv_b, w_r, b_r, w_i, b_i, lam):
    b, s, width = p_c.shape
    t = T_LRU
    row = lambda a: a.reshape(1, -1)
    const = lambda shape: pl.BlockSpec(shape, lambda bi, ci: (0,) * len(shape))
    return pl.pallas_call(
        _lru_kernel, grid=(b, s // t),
        in_specs=[pl.BlockSpec((1, t, width), lambda bi, ci: (bi, ci, 0)),
                  const((CONV_W, G)), const((1, G)), const((G, G)), const((1, G)),
                  const((G, G)), const((1, G)), const((1, G))],
        out_specs=pl.BlockSpec((1, t, G), lambda bi, ci: (bi, ci, 0)),
        out_shape=jax.ShapeDtypeStruct((b, s, G), BF16),
        scratch_shapes=[pltpu.VMEM((SUBLANES, G), F32), pltpu.VMEM((1, G), F32)],
        compiler_params=pltpu.CompilerParams(dimension_semantics=("parallel", "arbitrary"),
                                             vmem_limit_bytes=VMEM_LIMIT),
        name="rglru",
    )(p_c, conv_w, row(conv_b), _block_diag_weight(w_r).astype(BF16), row(b_r),
      _block_diag_weight(w_i).astype(BF16), row(b_i), row(lam))


def _ret_kernel(p_ref, cos_ref, sin_ref, dm_ref, xi_ref, zeta_ref, g_ref, nw_ref, o_ref, r_ref):
    t = p_ref.shape[1]

    @pl.when(pl.program_id(1) == 0)
    def _():
        r_ref[...] = jnp.zeros_like(r_ref)

    sb = dm_ref.shape[1]
    lane = lax.broadcasted_iota(jnp.int32, (sb, LANES), 1)
    m_e = lane < HEAD_DIM
    first_half = (lane % HEAD_DIM) < (HEAD_DIM // 2)
    bd = _block_diag_mask()
    ones2 = _head_ones()

    def rope(x, rw):
        swapped = jnp.where(first_half, pltpu.roll(x, LANES - HEAD_DIM // 2, axis=1),
                            pltpu.roll(x, HEAD_DIM // 2, axis=1))
        return x * cos_ref[rw, :] + swapped * sin_ref[rw, :]

    pairs = range(N_PAIRS)
    sls = [slice(p * LANES, (p + 1) * LANES) for p in pairs]
    rws = [slice(i * sb, (i + 1) * sb) for i in range(t // sb)]
    cp = [(rw, p) for rw in rws for p in pairs]
    ch = [(i, half) for i in range(len(cp)) for half in range(2)]
    col = lambda base, p: slice(base + p * LANES, base + (p + 1) * LANES)
    q = [rope(p_ref[0, rw, col(0, p)].astype(F32), rw) for rw, p in cp]
    k = [rope(p_ref[0, rw, col(G, p)].astype(F32), rw) * (HEAD_DIM ** -0.5) for rw, p in cp]
    vb = [p_ref[0, rw, col(2 * G, p)] for rw, p in cp]
    kb = [x.astype(BF16) for x in k]
    qm = [(jnp.where(m_e, q[i], 0.0) if half == 0 else jnp.where(m_e, 0.0, q[i])).astype(BF16)
          for i, half in ch]
    s = [_dot_nt(qm[c], kb[i]) * dm_ref[2 * cp[i][1] + half] for c, (i, half) in enumerate(ch)]
    o_h = [_dot(s[c], vb[i]) for c, (i, _) in enumerate(ch)]
    kv = [_dot_tn(k[i] * zeta_ref[p], vb[i]) for i, (_, p) in enumerate(cp)]
    r_st = [r_ref[p] for p in pairs]
    q_r = []
    for i, (_, p) in enumerate(cp):
        q_r.append(_dot(q[i], r_st[p]))
        r_st[p] = g_ref[p] * r_st[p] + jnp.where(bd, kv[i], 0.0)
    for p in pairs:
        r_ref[p] = r_st[p]
    for i, (rw, p) in enumerate(cp):
        o = jnp.where(m_e, o_h[2 * i], o_h[2 * i + 1]) + xi_ref[p] * q_r[i]
        z = p_ref[0, rw, col(3 * G, p)].astype(F32)
        o_ref[0, rw, sls[p]] = (_pair_layer_norm(o, ones2) * nw_ref[:, sls[p]] * _silu(z)).astype(o_ref.dtype)


def _ret_tables(s):
    t = RET_SB
    half = HEAD_DIM // 2
    pos = jnp.arange(s, dtype=F32)
    inv_freq = ROPE_THETA ** (-jnp.arange(half, dtype=F32) / half)
    ang = pos[:, None] * inv_freq[None, :]
    cos = jnp.tile(jnp.cos(ang), (1, LANES // half))
    sin = jnp.sin(ang)
    sin = jnp.tile(jnp.concatenate([-sin, sin], axis=-1), (1, LANES // HEAD_DIM))
    log_g = jnp.log1p(-jnp.exp2(-5.0 - jnp.arange(N_HEADS, dtype=F32)))
    idx = jnp.arange(t, dtype=F32)
    chunk = jnp.arange(t) // RET_CHUNK
    visible = chunk[:, None] >= chunk[None, :]
    dm = jnp.where(visible[None], jnp.exp(log_g[:, None, None] * jnp.abs(idx[:, None] - idx[None, :])), 0.0)
    pair_lanes = lambda a: jnp.repeat(a.reshape(N_PAIRS, 2, -1), HEAD_DIM, axis=1).transpose(0, 2, 1)
    xi = pair_lanes(jnp.exp(log_g[:, None] * (idx + 1.0)))
    zeta = pair_lanes(jnp.exp(log_g[:, None] * (t - 1.0 - idx)))
    g_blk = pair_lanes(jnp.exp(log_g * t)[:, None])
    return cos, sin, dm, xi, zeta, g_blk


def _retention(p_d, norm_w, tables):
    b, s, width = p_d.shape
    t = T_RET
    cos, sin, dm, xi, zeta, g_blk = tables
    const = lambda shape: pl.BlockSpec(shape, lambda bi, ci: (0,) * len(shape))
    return pl.pallas_call(
        _ret_kernel, grid=(b, s // t),
        in_specs=[pl.BlockSpec((1, t, width), lambda bi, ci: (bi, ci, 0)),
                  pl.BlockSpec((t, LANES), lambda bi, ci: (ci, 0)),
                  pl.BlockSpec((t, LANES), lambda bi, ci: (ci, 0)),
                  const((N_HEADS, RET_SB, RET_SB)), const((N_PAIRS, RET_SB, LANES)),
                  const((N_PAIRS, RET_SB, LANES)), const((N_PAIRS, 1, LANES)), const((1, G))],
        out_specs=pl.BlockSpec((1, t, G), lambda bi, ci: (bi, ci, 0)),
        out_shape=jax.ShapeDtypeStruct((b, s, G), BF16),
        scratch_shapes=[pltpu.VMEM((N_PAIRS, LANES, LANES), F32)],
        compiler_params=pltpu.CompilerParams(dimension_semantics=("parallel", "arbitrary"),
                                             vmem_limit_bytes=VMEM_LIMIT),
        name="retention",
    )(p_d, cos, sin, dm, xi, zeta, g_blk, norm_w.reshape(1, -1))


def kernel(x, norm_pre, norm_post, w_in, w_out, mlstm_conv_w, mlstm_conv_b, mlstm_i_bias, mlstm_f_bias, mlstm_norm_w, rwkv_mu, rwkv_w_up, rwkv_w0, rwkv_a_up, rwkv_a0, rwkv_k_k, rwkv_k_a, rwkv_r_k, rwkv_gn_w, rwkv_gn_b, lru_conv_w, lru_conv_b, lru_w_r, lru_b_r, lru_w_i, lru_b_i, lru_lambda, ret_norm_w):
    b, s, d = x.shape
    depth = w_in.shape[0]
    tables = _ret_tables(s)
    x2 = x.reshape(b * s, d)
    for l in range(depth):
        w = w_in[l]
        gain = norm_pre[l].reshape(1, d)
        w_a = jnp.concatenate([w[:, _A0:_B0], jnp.zeros((d, LANES - 2 * N_HEADS), F32)],
                              axis=1).astype(BF16)
        p_a, gates = _proj(x2, gain, w_a, [(_AG, BF16), (LANES, F32)], "in_proj_mlstm")
        p_b, = _proj(x2, gain, w[:, _B0:_C0].astype(BF16), [(_C0 - _B0, BF16)], "in_proj_rwkv7")
        p_c, = _proj(x2, gain, w[:, _C0:_D0].astype(BF16), [(_D0 - _C0, BF16)], "in_proj_rglru")
        p_d, = _proj(x2, gain, w[:, _D0:_END].astype(BF16), [(_END - _D0, BF16)], "in_proj_retention")

        y_a = _mlstm(p_a.reshape(b, s, -1), gates.reshape(b, s, -1), mlstm_conv_w[l], mlstm_conv_b[l],
                     mlstm_i_bias[l], mlstm_f_bias[l], mlstm_norm_w[l])
        y_b = _rwkv(p_b.reshape(b, s, -1), rwkv_mu[l], rwkv_w_up[l], rwkv_w0[l], rwkv_a_up[l],
                    rwkv_a0[l], rwkv_k_k[l], rwkv_k_a[l], rwkv_r_k[l], rwkv_gn_w[l], rwkv_gn_b[l])
        y_c = _lru(p_c.reshape(b, s, -1), lru_conv_w[l], lru_conv_b[l], lru_w_r[l], lru_b_r[l],
                   lru_w_i[l], lru_b_i[l], lru_lambda[l])
        y_d = _retention(p_d.reshape(b, s, -1), ret_norm_w[l], tables)

        ys = [y.reshape(b * s, G) for y in (y_a, y_b, y_c, y_d)]
        x2 = _out_proj(ys, w_out[l].reshape(4, G, d).astype(BF16), norm_post[l].reshape(1, d), x2)
    return x2.reshape(b, s, d)
```

```python
import functools

import numpy as np
import jax
import jax.numpy as jnp
from jax import lax
from jax.experimental import pallas as pl
from jax.experimental.pallas import tpu as pltpu

F32 = jnp.float32
BF16 = jnp.bfloat16

D_MODEL = 1024
G = 512
N_HEADS = 8
HEAD_DIM = 64
N_PAIRS = N_HEADS // 2
LANES = 128
SUBLANES = 8
CONV_W = 4
LORA = 64
LRU_C = 8.0
W_DECAY_SCALE = 0.606531
ROPE_THETA = 10000.0
NORM_EPS = 1e-6
GN_EPS = 1e-5
RET_CHUNK = 64
NEG = -1e30

_A0, _AG, _B0, _C0, _D0, _END = 0, 2560, 2576, 4752, 5776, 7824

MLSTM_CHUNK = 128
T_MLSTM = 512
T_RET = 512
RET_SB = 128
RWKV_CHUNK = 64
T_RWKV = 256
T_LRU = 512
TM_PROJ = 1024
VMEM_LIMIT = 48 * 1024 * 1024


def _sigmoid(x):
    return jax.nn.sigmoid(x)


def _silu(x):
    return x * jax.nn.sigmoid(x)


def _softplus(x):
    return jnp.maximum(x, 0.0) + jnp.log1p(jnp.exp(-jnp.abs(x)))


def _log_sigmoid(x):
    return -_softplus(-x)


def _dot(a, b):
    return jnp.dot(a.astype(BF16), b.astype(BF16), preferred_element_type=F32)


def _dot_nt(a, b):
    return lax.dot_general(a.astype(BF16), b.astype(BF16), (((1,), (1,)), ((), ())),
                           preferred_element_type=F32)


def _dot_tn(a, b):
    return lax.dot_general(a.astype(BF16), b.astype(BF16), (((0,), (0,)), ((), ())),
                           preferred_element_type=F32)


def _shift_rows(x, prev8, j):
    xr = pltpu.roll(x, j, axis=0)
    pr = pltpu.roll(prev8, j, axis=0)
    row = lax.broadcasted_iota(jnp.int32, prev8.shape, 0)
    first = jnp.where(row < j, pr, xr[0:SUBLANES])
    return jnp.concatenate([first, xr[SUBLANES:]], axis=0)


def _cumsum(x, axis):
    n = x.shape[axis]
    idx = lax.broadcasted_iota(jnp.int32, x.shape, axis)
    d = 1
    while d < n:
        x = x + jnp.where(idx >= d, pltpu.roll(x, d, axis=axis), 0.0)
        d *= 2
    return x


def _causal_conv(xb, prev8, sh_ref, w_ref, b_ref):
    row8 = lax.broadcasted_iota(jnp.int32, prev8.shape, 0)
    y = xb.astype(F32) * w_ref[CONV_W - 1:CONV_W, :] + b_ref[...]
    for j in range(1, CONV_W):
        xs = jnp.dot(sh_ref[j - 1], xb, preferred_element_type=F32)
        head = xs[0:SUBLANES] + jnp.where(row8 < j, pltpu.roll(prev8, j, axis=0), 0.0)
        xs = jnp.concatenate([head, xs[SUBLANES:]], axis=0)
        y = y + xs * w_ref[CONV_W - 1 - j:CONV_W - j, :]
    return y


def _causal_conv_rolls(x, prev8, w_ref, b_ref):
    y = x * w_ref[CONV_W - 1:CONV_W, :] + b_ref[...]
    for j in range(1, CONV_W):
        y = y + _shift_rows(x, prev8, j) * w_ref[CONV_W - 1 - j:CONV_W - j, :]
    return y


def _shift_matrices(t):
    r = np.arange(t)[:, None]
    c = np.arange(t)[None, :]
    return jnp.asarray(np.stack([(r - c == j) for j in range(1, CONV_W)]).astype(np.float32)).astype(BF16)


def _split2(x):
    hi = x.astype(BF16)
    lo = (x - hi.astype(F32)).astype(BF16)
    return jnp.concatenate([hi, lo], axis=1)


def _head_ones():
    r = lax.broadcasted_iota(jnp.int32, (2 * LANES, LANES), 0) % LANES
    c = lax.broadcasted_iota(jnp.int32, (2 * LANES, LANES), 1)
    return ((r < HEAD_DIM) == (c < HEAD_DIM)).astype(BF16)


def _pair_sum(x, ones2):
    return jnp.dot(_split2(x), ones2, preferred_element_type=F32)


def _pair_sum_lanes(x, m_e):
    se = jnp.sum(jnp.where(m_e, x, 0.0), axis=-1, keepdims=True)
    so = jnp.sum(jnp.where(m_e, 0.0, x), axis=-1, keepdims=True)
    return jnp.where(m_e, se, so)


def _pair_layer_norm(x, ones2):
    mu = _pair_sum(x, ones2) * (1.0 / HEAD_DIM)
    xc = x - mu
    var = _pair_sum(xc * xc, ones2) * (1.0 / HEAD_DIM)
    return xc * lax.rsqrt(var + GN_EPS)


def _block_diag_mask():
    r = lax.broadcasted_iota(jnp.int32, (LANES, LANES), 0)
    c = lax.broadcasted_iota(jnp.int32, (LANES, LANES), 1)
    return (r < HEAD_DIM) == (c < HEAD_DIM)


def _proj_kernel(x_ref, g_ref, w_ref, *o_refs):
    x = x_ref[...]
    rs = lax.rsqrt(jnp.mean(x * x, axis=-1, keepdims=True) + NORM_EPS)
    acc = jnp.dot((x * g_ref[...]).astype(BF16), w_ref[...], preferred_element_type=F32) * rs
    off = 0
    for o_ref in o_refs:
        n = o_ref.shape[1]
        o_ref[...] = acc[:, off:off + n].astype(o_ref.dtype)
        off += n


def _proj(x2, gain, w, outs, name):
    m, d = x2.shape
    n = w.shape[1]
    tm = TM_PROJ
    res = pl.pallas_call(
        _proj_kernel, grid=(m // tm,),
        in_specs=[pl.BlockSpec((tm, d), lambda i: (i, 0)),
                  pl.BlockSpec((1, d), lambda i: (0, 0)),
                  pl.BlockSpec((d, n), lambda i: (0, 0))],
        out_specs=[pl.BlockSpec((tm, wd), lambda i: (i, 0)) for wd, _ in outs],
        out_shape=[jax.ShapeDtypeStruct((m, wd), dt) for wd, dt in outs],
        compiler_params=pltpu.CompilerParams(dimension_semantics=("parallel",),
                                             vmem_limit_bytes=VMEM_LIMIT),
        name=name,
    )(x2, gain, w)
    return res


def _out_kernel(ya_ref, yb_ref, yc_ref, yd_ref, w_ref, g_ref, x_ref, o_ref):
    acc = jnp.dot(ya_ref[...].astype(BF16), w_ref[0], preferred_element_type=F32)
    acc += jnp.dot(yb_ref[...].astype(BF16), w_ref[1], preferred_element_type=F32)
    acc += jnp.dot(yc_ref[...].astype(BF16), w_ref[2], preferred_element_type=F32)
    acc += jnp.dot(yd_ref[...].astype(BF16), w_ref[3], preferred_element_type=F32)
    ms = jnp.mean(acc * acc, axis=-1, keepdims=True)
    o_ref[...] = x_ref[...] + (acc * lax.rsqrt(ms + NORM_EPS)) * g_ref[...]


def _out_proj(ys, w4, gain, x2):
    m, d = x2.shape
    tm = TM_PROJ
    yspec = pl.BlockSpec((tm, G), lambda i: (i, 0))
    return pl.pallas_call(
        _out_kernel, grid=(m // tm,),
        in_specs=[yspec, yspec, yspec, yspec,
                  pl.BlockSpec((4, G, d), lambda i: (0, 0, 0)),
                  pl.BlockSpec((1, d), lambda i: (0, 0)),
                  pl.BlockSpec((tm, d), lambda i: (i, 0))],
        out_specs=pl.BlockSpec((tm, d), lambda i: (i, 0)),
        out_shape=jax.ShapeDtypeStruct((m, d), F32),
        compiler_params=pltpu.CompilerParams(dimension_semantics=("parallel",),
                                             vmem_limit_bytes=VMEM_LIMIT),
        name="out_proj",
    )(*ys, w4, gain, x2)


def _mlstm_kernel(p_ref, g_ref, sh_ref, cw_ref, cb_ref, bcol_ref, nw_ref, o_ref,
                  prev_ref, c_ref, n_ref, m_ref):
    t = p_ref.shape[1]

    @pl.when(pl.program_id(1) == 0)
    def _():
        prev_ref[...] = jnp.zeros_like(prev_ref)
        c_ref[...] = jnp.zeros_like(c_ref)
        n_ref[...] = jnp.zeros_like(n_ref)
        m_ref[...] = jnp.zeros_like(m_ref)

    sb = sh_ref.shape[1]
    nsb = t // sb
    rws = [slice(i * sb, (i + 1) * sb) for i in range(nsb)]
    tail8 = lambda i: p_ref[0, (i + 1) * sb - 2 * SUBLANES:(i + 1) * sb, 0:2 * G].astype(F32)[SUBLANES:]
    qk = [_silu(_causal_conv(p_ref[0, rws[i], 0:2 * G], prev_ref[...] if i == 0 else tail8(i - 1),
                             sh_ref, cw_ref, cb_ref)) for i in range(nsb)]
    prev_ref[...] = tail8(nsb - 1)

    lane = lax.broadcasted_iota(jnp.int32, (sb, LANES), 1)
    m_e = lane < HEAD_DIM
    ones2 = _head_ones()
    ones_row = jnp.ones((2 * sb, LANES), BF16)

    gcol = [g_ref[0, rw, :] + bcol_ref[...] for rw in rws]
    cumc = [_cumsum(_log_sigmoid(gc), 0) for gc in gcol]
    grow = [jnp.where(lane < N_HEADS, gcol[i], cumc[i]).T for i in range(nsb)]

    m_e1 = lax.broadcasted_iota(jnp.int32, (1, LANES), 1) < HEAD_DIM
    tril = (lax.broadcasted_iota(jnp.int32, (sb, sb), 0) >= lax.broadcasted_iota(jnp.int32, (sb, sb), 1))
    bd = _block_diag_mask()

    pairs = range(N_PAIRS)
    sls = [slice(p * LANES, (p + 1) * LANES) for p in pairs]
    cp = [(i, p) for i in range(nsb) for p in pairs]
    ch = [(i, h) for i in range(nsb) for h in range(N_HEADS)]
    ip = lambda i, p: i * N_PAIRS + p
    ih = lambda i, h: i * N_HEADS + h
    sel = lambda xs, i, p: jnp.where(m_e, xs[ih(i, 2 * p)], xs[ih(i, 2 * p + 1)])
    sel1 = lambda xs, i, p: jnp.where(m_e1, xs[ih(i, 2 * p)], xs[ih(i, 2 * p + 1)])
    col = lambda base, p: slice(base + p * LANES, base + (p + 1) * LANES)
    nch = range(len(ch))

    q = [qk[i][:, sls[p]] for i, p in cp]
    k = [qk[i][:, col(G, p)] * (HEAD_DIM ** -0.5) for i, p in cp]
    vb = [p_ref[0, rws[i], col(2 * G, p)] for i, p in cp]
    kb = [x.astype(BF16) for x in k]
    qm = [(jnp.where(m_e, q[ip(i, h // 2)], 0.0) if h % 2 == 0
           else jnp.where(m_e, 0.0, q[ip(i, h // 2)])).astype(BF16) for i, h in ch]
    s_raw = [_dot_nt(qm[c], kb[ip(i, h // 2)]) for c, (i, h) in enumerate(ch)]

    cum_c = [cumc[i][:, N_HEADS + h:N_HEADS + h + 1] for i, h in ch]
    i_c = [gcol[i][:, h:h + 1] for i, h in ch]
    tot = [cum_c[c][sb - 1:sb, :] for c in nch]
    log_w = [tot[c] - cum_c[c] + i_c[c] for c in nch]
    lw_max = [jnp.max(log_w[c], axis=0, keepdims=True) for c in nch]
    m_st = [m_ref[p] for p in pairs]
    m_prev, m_new = [], []
    for c, (i, h) in enumerate(ch):
        mp = (m_st[h // 2][:, (h % 2) * HEAD_DIM:(h % 2) * HEAD_DIM + 1] if i == 0
              else m_new[ih(i - 1, h)])
        m_prev.append(mp)
        m_new.append(jnp.maximum(tot[c] + mp, lw_max[c]))
    log_d = [jnp.where(tril, cum_c[c] - grow[i][N_HEADS + h:N_HEADS + h + 1, :] + grow[i][h:h + 1, :], NEG)
             for c, (i, h) in enumerate(ch)]
    inter = [cum_c[c] + m_prev[c] for c in nch]
    m_t = [jnp.maximum(inter[c], jnp.max(log_d[c], axis=-1, keepdims=True)) for c in nch]
    s = [s_raw[c] * jnp.exp(log_d[c] - m_t[c]) for c in nch]
    s2 = [_split2(s[c]) for c in nch]
    num_h = [jnp.dot(s2[c][:, 0:sb], vb[ip(i, h // 2)], preferred_element_type=F32)
             for c, (i, h) in enumerate(ch)]
    den_h = [jnp.dot(s2[c], ones_row, preferred_element_type=F32) for c in nch]
    s_int_h = [jnp.exp(inter[c] - m_t[c]) for c in nch]
    wj_h = [jnp.exp(log_w[c] - m_new[c]) for c in nch]
    sc_h = [jnp.exp(tot[c] + m_prev[c] - m_new[c]) for c in nch]

    kw = [k[ip(i, p)] * sel(wj_h, i, p) for i, p in cp]
    vk = [_dot_tn(vb[x], kw[x]) for x in range(len(cp))]
    kw_sum = [jnp.sum(kw[x], axis=0, keepdims=True) for x in range(len(cp))]

    c_st = [c_ref[p] for p in pairs]
    n_st = [n_ref[p] for p in pairs]
    q_c, qn = [], []
    for x, (i, p) in enumerate(cp):
        q_c.append(_dot_nt(q[x], c_st[p]))
        qn.append(_pair_sum(q[x] * n_st[p], ones2))
        sc = sel1(sc_h, i, p)
        c_st[p] = sc * c_st[p] + jnp.where(bd, vk[x], 0.0)
        n_st[p] = sc * n_st[p] + kw_sum[x]
    for p in pairs:
        c_ref[p] = c_st[p]
        n_ref[p] = n_st[p]
        m_ref[p] = sel1(m_new, nsb - 1, p)

    for x, (i, p) in enumerate(cp):
        s_int = sel(s_int_h, i, p)
        num = sel(num_h, i, p) + s_int * q_c[x]
        den = sel(den_h, i, p) + s_int * qn[x]
        h_out = num / jnp.maximum(jnp.abs(den), jnp.exp(-sel(m_t, i, p)))
        o_gate = _sigmoid(p_ref[0, rws[i], col(3 * G, p)].astype(F32))
        z = p_ref[0, rws[i], col(4 * G, p)].astype(F32)
        y = _pair_layer_norm(h_out * o_gate, ones2) * nw_ref[:, sls[p]]
        o_ref[0, rws[i], sls[p]] = (y * _silu(z)).astype(o_ref.dtype)


def _mlstm(p_a, gates, conv_w, conv_b, i_bias, f_bias, norm_w):
    b, s, width = p_a.shape
    t = T_MLSTM
    nc = s // t
    bcol = jnp.zeros((1, LANES), F32).at[0, 0:N_HEADS].set(i_bias).at[0, N_HEADS:2 * N_HEADS].set(f_bias)
    const = lambda shape: pl.BlockSpec(shape, lambda bi, ci: (0,) * len(shape))
    return pl.pallas_call(
        _mlstm_kernel, grid=(b, nc),
        in_specs=[pl.BlockSpec((1, t, width), lambda bi, ci: (bi, ci, 0)),
                  pl.BlockSpec((1, t, LANES), lambda bi, ci: (bi, ci, 0)),
                  const((CONV_W - 1, MLSTM_CHUNK, MLSTM_CHUNK)),
                  const((CONV_W, 2 * G)), const((1, 2 * G)), const((1, LANES)), const((1, G))],
        out_specs=pl.BlockSpec((1, t, G), lambda bi, ci: (bi, ci, 0)),
        out_shape=jax.ShapeDtypeStruct((b, s, G), BF16),
        scratch_shapes=[pltpu.VMEM((SUBLANES, 2 * G), F32),
                        pltpu.VMEM((N_PAIRS, LANES, LANES), F32),
                        pltpu.VMEM((N_PAIRS, 1, LANES), F32),
                        pltpu.VMEM((N_PAIRS, 1, LANES), F32)],
        compiler_params=pltpu.CompilerParams(dimension_semantics=("parallel", "arbitrary"),
                                             vmem_limit_bytes=VMEM_LIMIT),
        name="mlstm",
    )(p_a, gates, _shift_matrices(MLSTM_CHUNK), conv_w, conv_b.reshape(1, -1), bcol, norm_w.reshape(1, -1))


def _stack(x, m_e):
    return jnp.concatenate([jnp.where(m_e, x, 0.0), jnp.where(m_e, 0.0, x)], axis=0)


def _rwkv_kernel(p_ref, mu_ref, wlo_ref, b0_ref, kk_ref, ka_ref, rk_ref, gw_ref, gb_ref,
                 msk_ref, o_ref, prev_ref, s_ref):
    t = p_ref.shape[1]
    sh_w = 3 * G + 2 * LORA

    @pl.when(pl.program_id(1) == 0)
    def _():
        prev_ref[...] = jnp.zeros_like(prev_ref)
        s_ref[...] = jnp.zeros_like(s_ref)

    x = p_ref[0, :, 0:sh_w].astype(F32)
    xs = x + mu_ref[...] * (_shift_rows(x, prev_ref[...], 1) - x)
    prev_ref[...] = x[t - SUBLANES:t, :]

    r = xs[:, 0:G]
    k = xs[:, G:2 * G]
    v = xs[:, 2 * G:3 * G]
    lo = xs[:, 3 * G:sh_w]
    lane = lax.broadcasted_iota(jnp.int32, (t, LANES), 1)
    m_e = lane < HEAD_DIM
    lo = jnp.where(m_e, jnp.tanh(lo), lo)
    pre = _dot(lo, wlo_ref[...]) + b0_ref[...]
    log_w = -W_DECAY_SCALE * _sigmoid(pre[:, 0:G])
    a = _sigmoid(pre[:, G:2 * G])

    kappa = k * kk_ref[...]
    k_til = k * (1.0 + (a - 1.0) * ka_ref[...])
    rkk = r * rk_ref[...] * k_til
    kap2 = kappa * kappa

    strict = msk_ref[0] > 0.5
    incl = msk_ref[1] > 0.5
    eye = msk_ref[2]

    ch = RWKV_CHUNK
    n2 = 2 * ch
    m_c = lax.broadcasted_iota(jnp.int32, (ch, LANES), 1) < HEAD_DIM
    pairs = range(N_PAIRS)
    chunks = range(t // ch)
    chains = [(j, p) for j in chunks for p in pairs]
    rows = [slice(j * ch, (j + 1) * ch) for j in chunks]
    sls = [slice(p * LANES, (p + 1) * LANES) for p in pairs]
    blk = lambda arr, c: arr[rows[c[0]], sls[c[1]]]

    log_p = [_cumsum(log_w[rw], 0) for rw in rows]
    p_in = [jnp.exp(lp) for lp in log_p]
    p_inv = [jnp.exp(-lp) for lp in log_p]
    p_prev = [jnp.exp(lp - log_w[rw]) for lp, rw in zip(log_p, rows)]
    p_end = [jnp.exp(lp[ch - 1:ch] - lp) for lp in log_p]
    p_l = [jnp.exp(lp[ch - 1:ch]) for lp in log_p]
    dec = lambda arr, c: arr[c[0]][:, sls[c[1]]]

    kh = [blk(kappa, c) * lax.rsqrt(_pair_sum_lanes(blk(kap2, c), m_c) + 1e-12) for c in chains]
    b_v = [blk(a, c) * kh[i] for i, c in enumerate(chains)]
    stk = lambda val: _stack(val, m_c).astype(BF16)
    khs = [stk(kh[i] * dec(p_prev, c)) for i, c in enumerate(chains)]
    rs = [stk(blk(r, c) * dec(p_in, c)) for c in chains]
    bs = [stk(b_v[i] * dec(p_inv, c)) for i, c in enumerate(chains)]
    kts = [stk(blk(k_til, c) * dec(p_inv, c)) for c in chains]
    vs = [stk(blk(v, c)) for c in chains]
    kte = [stk(blk(k_til, c) * dec(p_end, c)) for c in chains]
    be = [stk(b_v[i] * dec(p_end, c)) for i, c in enumerate(chains)]
    idx = range(len(chains))

    g = [_dot_nt(jnp.concatenate([khs[i], rs[i]], axis=0), jnp.concatenate([bs[i], kts[i]], axis=0))
         for i in idx]
    a_ub = [jnp.where(strict, g[i][0:n2, 0:n2], 0.0) for i in idx]
    a_k = [jnp.concatenate([jnp.where(strict, g[i][0:n2, n2:], 0.0),
                            jnp.where(incl, g[i][n2:, n2:], 0.0)], axis=0).astype(BF16) for i in idx]
    a_rb = [jnp.where(incl, g[i][n2:, 0:n2], 0.0).astype(BF16) for i in idx]
    av = [_dot(a_k[i], vs[i]) for i in idx]

    xinv = [eye - a_ub[i] * msk_ref[3] for i in idx]
    for lvl in range(4, 9):
        xa = [_dot(xinv[i], a_ub[i] * msk_ref[lvl]) for i in idx]
        xinv = [xinv[i] - _dot(xa[i], xinv[i]) for i in idx]
    wu = [_dot(xinv[i], jnp.concatenate([khs[i], av[i][0:n2].astype(BF16)], axis=1)) for i in idx]

    bonus_sum = [_pair_sum_lanes(blk(rkk, c), m_c) for c in chains]
    s_st = [s_ref[p] for p in pairs]
    ys = []
    for j in chunks:
        ids = [j * N_PAIRS + p for p in pairs]
        ws = [_dot_nt(jnp.concatenate([wu[i][:, 0:LANES].astype(BF16), rs[i]], axis=0), s_st[p])
              for p, i in enumerate(ids)]
        ub = [(-(ws[p][0:n2] + wu[i][:, LANES:])).astype(BF16) for p, i in enumerate(ids)]
        ys += [ws[p][n2:] + av[i][n2:] + _dot(a_rb[i], ub[p]) for p, i in enumerate(ids)]
        s_st = [s_st[p] * p_l[j][:, sls[p]] + _dot_tn(jnp.concatenate([vs[i], ub[p]], axis=0),
                                                      jnp.concatenate([kte[i], be[i]], axis=0))
                for p, i in enumerate(ids)]
    for p in pairs:
        s_ref[p] = s_st[p]

    wkv = [ys[i][0:ch] + ys[i][ch:n2] for i in idx]
    mu = [_pair_sum_lanes(wkv[i], m_c) * (1.0 / HEAD_DIM) for i in idx]
    xc = [wkv[i] - mu[i] for i in idx]
    var = [_pair_sum_lanes(xc[i] * xc[i], m_c) * (1.0 / HEAD_DIM) for i in idx]
    for i, (j, p) in enumerate(chains):
        z = p_ref[0, rows[j], sh_w + p * LANES:sh_w + (p + 1) * LANES].astype(F32)
        y = (xc[i] * lax.rsqrt(var[i] + GN_EPS)) * gw_ref[:, sls[p]] + gb_ref[:, sls[p]] \
            + bonus_sum[i] * blk(v, chains[i])
        o_ref[0, rows[j], sls[p]] = (y * _silu(z)).astype(o_ref.dtype)


def _rwkv_masks():
    n = 2 * RWKV_CHUNK
    r = np.arange(n)[:, None]
    c = np.arange(n)[None, :]
    ms = [r > c, r >= c, r == c]
    b = 1
    while b < RWKV_CHUNK:
        ms.append((r // (2 * b) == c // (2 * b)) & (r % (2 * b) >= b) & (c % (2 * b) < b))
        b *= 2
    return jnp.asarray(np.stack(ms).astype(np.float32))


def _rwkv(p_b, mu, w_up, w0, a_up, a0, k_k, k_a, r_k, gn_w, gn_b):
    b, s, width = p_b.shape
    t = T_RWKV
    sh_w = 3 * G + 2 * LORA
    w_lo = jnp.zeros((LANES, 2 * G), F32).at[0:LORA, 0:G].set(w_up).at[LORA:, G:].set(a_up).astype(BF16)
    b0 = jnp.concatenate([w0, a0]).reshape(1, 2 * G)
    row = lambda a: a.reshape(1, -1)
    const = lambda shape: pl.BlockSpec(shape, lambda bi, ci: (0,) * len(shape))
    return pl.pallas_call(
        _rwkv_kernel, grid=(b, s // t),
        in_specs=[pl.BlockSpec((1, t, width), lambda bi, ci: (bi, ci, 0)),
                  const((1, sh_w)), const((LANES, 2 * G)), const((1, 2 * G)),
                  const((1, G)), const((1, G)), const((1, G)), const((1, G)), const((1, G)),
                  const((9, 2 * RWKV_CHUNK, 2 * RWKV_CHUNK))],
        out_specs=pl.BlockSpec((1, t, G), lambda bi, ci: (bi, ci, 0)),
        out_shape=jax.ShapeDtypeStruct((b, s, G), BF16),
        scratch_shapes=[pltpu.VMEM((SUBLANES, sh_w), F32),
                        pltpu.VMEM((N_PAIRS, LANES, LANES), F32)],
        compiler_params=pltpu.CompilerParams(dimension_semantics=("parallel", "arbitrary"),
                                             vmem_limit_bytes=VMEM_LIMIT),
        name="rwkv7",
    )(p_b, row(mu), w_lo, b0, row(k_k), row(k_a), row(r_k), row(gn_w), row(gn_b), _rwkv_masks())


def _lru_kernel(p_ref, cw_ref, cb_ref, wr_ref, br_ref, wi_ref, bi_ref, lam_ref, o_ref,
                prev_ref, h_ref):
    t = p_ref.shape[1]

    @pl.when(pl.program_id(1) == 0)
    def _():
        prev_ref[...] = jnp.zeros_like(prev_ref)
        h_ref[...] = jnp.zeros_like(h_ref)

    x = p_ref[0, :, 0:G].astype(F32)
    xc = _causal_conv_rolls(x, prev_ref[...], cw_ref, cb_ref)
    prev_ref[...] = x[t - SUBLANES:t, :]

    xb = xc.astype(BF16)
    r = _sigmoid(jnp.dot(xb, wr_ref[...], preferred_element_type=F32) + br_ref[...])
    i = _sigmoid(jnp.dot(xb, wi_ref[...], preferred_element_type=F32) + bi_ref[...])
    log_a = (-LRU_C * r) * _softplus(-lam_ref[...])
    a = jnp.exp(log_a)
    th = jnp.tanh(log_a)
    u = jnp.sqrt(-2.0 * th / (1.0 - th)) * (i * xc)

    sub = lax.broadcasted_iota(jnp.int32, (SUBLANES, G), 0)
    carry = h_ref[...]
    groups = []
    for i in range(t // SUBLANES):
        rows = slice(i * SUBLANES, (i + 1) * SUBLANES)
        ug, ag = u[rows], a[rows]
        d = 1
        while d < SUBLANES:
            keep = sub >= d
            ug = ug + ag * jnp.where(keep, pltpu.roll(ug, d, axis=0), 0.0)
            ag = ag * jnp.where(keep, pltpu.roll(ag, d, axis=0), 1.0)
            d *= 2
        hg = ug + ag * carry
        carry = hg[SUBLANES - 1:SUBLANES]
        groups.append(hg)
    h_ref[...] = carry
    h = jnp.concatenate(groups, axis=0)
    o_ref[0] = (h * _silu(p_ref[0, :, G:2 * G].astype(F32))).astype(o_ref.dtype)


def _block_diag_weight(w):
    eye = jnp.eye(N_HEADS, dtype=w.dtype)
    return jnp.einsum("hij,hg->higj", w, eye).reshape(G, G)


def _lru(p_c, conv_w, conv_b, w_r, b_r, w_i, b_i, lam):
    b, s, width = p_c.shape
    t = T_LRU
    row = lambda a: a.reshape(1, -1)
    const = lambda shape: pl.BlockSpec(shape, lambda bi, ci: (0,) * len(shape))
    return pl.pallas_call(
        _lru_kernel, grid=(b, s // t),
        in_specs=[pl.BlockSpec((1, t, width), lambda bi, ci: (bi, ci, 0)),
                  const((CONV_W, G)), const((1, G)), const((G, G)), const((1, G)),
                  const((G, G)), const((1, G)), const((1, G))],
        out_specs=pl.BlockSpec((1, t, G), lambda bi, ci: (bi, ci, 0)),
        out_shape=jax.ShapeDtypeStruct((b, s, G), BF16),
        scratch_shapes=[pltpu.VMEM((SUBLANES, G), F32), pltpu.VMEM((1, G), F32)],
        compiler_params=pltpu.CompilerParams(dimension_semantics=("parallel", "arbitrary"),
                                             vmem_limit_bytes=VMEM_LIMIT),
        name="rglru",
    )(p_c, conv_w, row(conv_b), _block_diag_weight(w_r).astype(BF16), row(b_r),
      _block_diag_weight(w_i).astype(BF16), row(b_i), row(lam))


def _ret_kernel(p_ref, cos_ref, sin_ref, dm_ref, xi_ref, zeta_ref, g_ref, nw_ref, o_ref, r_ref):
    t = p_ref.shape[1]

    @pl.when(pl.program_id(1) == 0)
    def _():
        r_ref[...] = jnp.zeros_like(r_ref)

    sb = dm_ref.shape[1]
    lane = lax.broadcasted_iota(jnp.int32, (sb, LANES), 1)
    m_e = lane < HEAD_DIM
    first_half = (lane % HEAD_DIM) < (HEAD_DIM // 2)
    bd = _block_diag_mask()
    ones2 = _head_ones()

    def rope(x, rw):
        swapped = jnp.where(first_half, pltpu.roll(x, LANES - HEAD_DIM // 2, axis=1),
                            pltpu.roll(x, HEAD_DIM // 2, axis=1))
        return x * cos_ref[rw, :] + swapped * sin_ref[rw, :]

    pairs = range(N_PAIRS)
    sls = [slice(p * LANES, (p + 1) * LANES) for p in pairs]
    rws = [slice(i * sb, (i + 1) * sb) for i in range(t // sb)]
    cp = [(rw, p) for rw in rws for p in pairs]
    ch = [(i, half) for i in range(len(cp)) for half in range(2)]
    col = lambda base, p: slice(base + p * LANES, base + (p + 1) * LANES)
    q = [rope(p_ref[0, rw, col(0, p)].astype(F32), rw) for rw, p in cp]
    k = [rope(p_ref[0, rw, col(G, p)].astype(F32), rw) * (HEAD_DIM ** -0.5) for rw, p in cp]
    vb = [p_ref[0, rw, col(2 * G, p)] for rw, p in cp]
    kb = [x.astype(BF16) for x in k]
    qm = [(jnp.where(m_e, q[i], 0.0) if half == 0 else jnp.where(m_e, 0.0, q[i])).astype(BF16)
          for i, half in ch]
    s = [_dot_nt(qm[c], kb[i]) * dm_ref[2 * cp[i][1] + half] for c, (i, half) in enumerate(ch)]
    o_h = [_dot(s[c], vb[i]) for c, (i, _) in enumerate(ch)]
    kv = [_dot_tn(k[i] * zeta_ref[p], vb[i]) for i, (_, p) in enumerate(cp)]
    r_st = [r_ref[p] for p in pairs]
    q_r = []
    for i, (_, p) in enumerate(cp):
        q_r.append(_dot(q[i], r_st[p]))
        r_st[p] = g_ref[p] * r_st[p] + jnp.where(bd, kv[i], 0.0)
    for p in pairs:
        r_ref[p] = r_st[p]
    for i, (rw, p) in enumerate(cp):
        o = jnp.where(m_e, o_h[2 * i], o_h[2 * i + 1]) + xi_ref[p] * q_r[i]
        z = p_ref[0, rw, col(3 * G, p)].astype(F32)
        o_ref[0, rw, sls[p]] = (_pair_layer_norm(o, ones2) * nw_ref[:, sls[p]] * _silu(z)).astype(o_ref.dtype)


def _ret_tables(s):
    t = RET_SB
    half = HEAD_DIM // 2
    pos = jnp.arange(s, dtype=F32)
    inv_freq = ROPE_THETA ** (-jnp.arange(half, dtype=F32) / half)
    ang = pos[:, None] * inv_freq[None, :]
    cos = jnp.tile(jnp.cos(ang), (1, LANES // half))
    sin = jnp.sin(ang)
    sin = jnp.tile(jnp.concatenate([-sin, sin], axis=-1), (1, LANES // HEAD_DIM))
    log_g = jnp.log1p(-jnp.exp2(-5.0 - jnp.arange(N_HEADS, dtype=F32)))
    idx = jnp.arange(t, dtype=F32)
    chunk = jnp.arange(t) // RET_CHUNK
    visible = chunk[:, None] >= chunk[None, :]
    dm = jnp.where(visible[None], jnp.exp(log_g[:, None, None] * jnp.abs(idx[:, None] - idx[None, :])), 0.0)
    pair_lanes = lambda a: jnp.repeat(a.reshape(N_PAIRS, 2, -1), HEAD_DIM, axis=1).transpose(0, 2, 1)
    xi = pair_lanes(jnp.exp(log_g[:, None] * (idx + 1.0)))
    zeta = pair_lanes(jnp.exp(log_g[:, None] * (t - 1.0 - idx)))
    g_blk = pair_lanes(jnp.exp(log_g * t)[:, None])
    return cos, sin, dm, xi, zeta, g_blk


def _retention(p_d, norm_w, tables):
    b, s, width = p_d.shape
    t = T_RET
    cos, sin, dm, xi, zeta, g_blk = tables
    const = lambda shape: pl.BlockSpec(shape, lambda bi, ci: (0,) * len(shape))
    return pl.pallas_call(
        _ret_kernel, grid=(b, s // t),
        in_specs=[pl.BlockSpec((1, t, width), lambda bi, ci: (bi, ci, 0)),
                  pl.BlockSpec((t, LANES), lambda bi, ci: (ci, 0)),
                  pl.BlockSpec((t, LANES), lambda bi, ci: (ci, 0)),
                  const((N_HEADS, RET_SB, RET_SB)), const((N_PAIRS, RET_SB, LANES)),
                  const((N_PAIRS, RET_SB, LANES)), const((N_PAIRS, 1, LANES)), const((1, G))],
        out_specs=pl.BlockSpec((1, t, G), lambda bi, ci: (bi, ci, 0)),
        out_shape=jax.ShapeDtypeStruct((b, s, G), BF16),
        scratch_shapes=[pltpu.VMEM((N_PAIRS, LANES, LANES), F32)],
        compiler_params=pltpu.CompilerParams(dimension_semantics=("parallel", "arbitrary"),
                                             vmem_limit_bytes=VMEM_LIMIT),
        name="retention",
    )(p_d, cos, sin, dm, xi, zeta, g_blk, norm_w.reshape(1, -1))


def kernel(x, norm_pre, norm_post, w_in, w_out, mlstm_conv_w, mlstm_conv_b, mlstm_i_bias, mlstm_f_bias, mlstm_norm_w, rwkv_mu, rwkv_w_up, rwkv_w0, rwkv_a_up, rwkv_a0, rwkv_k_k, rwkv_k_a, rwkv_r_k, rwkv_gn_w, rwkv_gn_b, lru_conv_w, lru_conv_b, lru_w_r, lru_b_r, lru_w_i, lru_b_i, lru_lambda, ret_norm_w):
    b, s, d = x.shape
    depth = w_in.shape[0]
    tables = _ret_tables(s)
    x2 = x.reshape(b * s, d)
    for l in range(depth):
        w = w_in[l]
        gain = norm_pre[l].reshape(1, d)
        w_a = jnp.concatenate([w[:, _A0:_B0], jnp.zeros((d, LANES - 2 * N_HEADS), F32)],
                              axis=1).astype(BF16)
        p_a, gates = _proj(x2, gain, w_a, [(_AG, BF16), (LANES, F32)], "in_proj_mlstm")
        p_b, = _proj(x2, gain, w[:, _B0:_C0].astype(BF16), [(_C0 - _B0, BF16)], "in_proj_rwkv7")
        p_c, = _proj(x2, gain, w[:, _C0:_D0].astype(BF16), [(_D0 - _C0, BF16)], "in_proj_rglru")
        p_d, = _proj(x2, gain, w[:, _D0:_END].astype(BF16), [(_END - _D0, BF16)], "in_proj_retention")

        y_a = _mlstm(p_a.reshape(b, s, -1), gates.reshape(b, s, -1), mlstm_conv_w[l], mlstm_conv_b[l],
                     mlstm_i_bias[l], mlstm_f_bias[l], mlstm_norm_w[l])
        y_b = _rwkv(p_b.reshape(b, s, -1), rwkv_mu[l], rwkv_w_up[l], rwkv_w0[l], rwkv_a_up[l],
                    rwkv_a0[l], rwkv_k_k[l], rwkv_k_a[l], rwkv_r_k[l], rwkv_gn_w[l], rwkv_gn_b[l])
        y_c = _lru(p_c.reshape(b, s, -1), lru_conv_w[l], lru_conv_b[l], lru_w_r[l], lru_b_r[l],
                   lru_w_i[l], lru_b_i[l], lru_lambda[l])
        y_d = _retention(p_d.reshape(b, s, -1), ret_norm_w[l], tables)

        ys = [y.reshape(b * s, G) for y in (y_a, y_b, y_c, y_d)]
        x2 = _out_proj(ys, w_out[l].reshape(4, G, d).astype(BF16), norm_post[l].reshape(1, d), x2)
    return x2.reshape(b, s, d)
```

```python
import functools

import numpy as np
import jax
import jax.numpy as jnp
from jax import lax
from jax.experimental import pallas as pl
from jax.experimental.pallas import tpu as pltpu

F32 = jnp.float32
BF16 = jnp.bfloat16

D_MODEL = 1024
G = 512
N_HEADS = 8
HEAD_DIM = 64
N_PAIRS = N_HEADS // 2
LANES = 128
SUBLANES = 8
CONV_W = 4
LORA = 64
LRU_C = 8.0
W_DECAY_SCALE = 0.606531
ROPE_THETA = 10000.0
NORM_EPS = 1e-6
GN_EPS = 1e-5
RET_CHUNK = 64
NEG = -1e30

_A0, _AG, _B0, _C0, _D0, _END = 0, 2560, 2576, 4752, 5776, 7824

MLSTM_CHUNK = 128
T_MLSTM = 512
T_RET = 512
RET_SB = 128
RWKV_CHUNK = 64
T_RWKV = 512
RWKV_GROUP = 16
T_LRU = 1024
TM_PROJ = 1024
VMEM_LIMIT = 48 * 1024 * 1024


def _sigmoid(x):
    return jax.nn.sigmoid(x)


def _silu(x):
    return x * jax.nn.sigmoid(x)


def _softplus(x):
    return jnp.maximum(x, 0.0) + jnp.log1p(jnp.exp(-jnp.abs(x)))


def _log_sigmoid(x):
    return -_softplus(-x)


def _dot(a, b):
    return jnp.dot(a.astype(BF16), b.astype(BF16), preferred_element_type=F32)


def _dot_nt(a, b):
    return lax.dot_general(a.astype(BF16), b.astype(BF16), (((1,), (1,)), ((), ())),
                           preferred_element_type=F32)


def _dot_tn(a, b):
    return lax.dot_general(a.astype(BF16), b.astype(BF16), (((0,), (0,)), ((), ())),
                           preferred_element_type=F32)


def _shift_rows(x, prev8, j):
    xr = pltpu.roll(x, j, axis=0)
    pr = pltpu.roll(prev8, j, axis=0)
    row = lax.broadcasted_iota(jnp.int32, prev8.shape, 0)
    first = jnp.where(row < j, pr, xr[0:SUBLANES])
    return jnp.concatenate([first, xr[SUBLANES:]], axis=0)


def _cumsum(x, axis):
    n = x.shape[axis]
    idx = lax.broadcasted_iota(jnp.int32, x.shape, axis)
    d = 1
    while d < n:
        x = x + jnp.where(idx >= d, pltpu.roll(x, d, axis=axis), 0.0)
        d *= 2
    return x


def _causal_conv(xb, prev8, sh_ref, w_ref, b_ref):
    row8 = lax.broadcasted_iota(jnp.int32, prev8.shape, 0)
    y = xb.astype(F32) * w_ref[CONV_W - 1:CONV_W, :] + b_ref[...]
    for j in range(1, CONV_W):
        xs = jnp.dot(sh_ref[j - 1], xb, preferred_element_type=F32)
        head = xs[0:SUBLANES] + jnp.where(row8 < j, pltpu.roll(prev8, j, axis=0), 0.0)
        xs = jnp.concatenate([head, xs[SUBLANES:]], axis=0)
        y = y + xs * w_ref[CONV_W - 1 - j:CONV_W - j, :]
    return y


def _causal_conv_rolls(x, prev8, w_ref, b_ref):
    y = x * w_ref[CONV_W - 1:CONV_W, :] + b_ref[...]
    for j in range(1, CONV_W):
        y = y + _shift_rows(x, prev8, j) * w_ref[CONV_W - 1 - j:CONV_W - j, :]
    return y


def _shift_matrices(t):
    r = np.arange(t)[:, None]
    c = np.arange(t)[None, :]
    return jnp.asarray(np.stack([(r - c == j) for j in range(1, CONV_W)]).astype(np.float32)).astype(BF16)


def _split2(x):
    hi = x.astype(BF16)
    lo = (x - hi.astype(F32)).astype(BF16)
    return jnp.concatenate([hi, lo], axis=1)


def _head_ones():
    r = lax.broadcasted_iota(jnp.int32, (2 * LANES, LANES), 0) % LANES
    c = lax.broadcasted_iota(jnp.int32, (2 * LANES, LANES), 1)
    return ((r < HEAD_DIM) == (c < HEAD_DIM)).astype(BF16)


def _pair_sum(x, ones2):
    return jnp.dot(_split2(x), ones2, preferred_element_type=F32)


def _pair_sum_lanes(x, m_e):
    se = jnp.sum(jnp.where(m_e, x, 0.0), axis=-1, keepdims=True)
    so = jnp.sum(jnp.where(m_e, 0.0, x), axis=-1, keepdims=True)
    return jnp.where(m_e, se, so)


def _pair_layer_norm(x, ones2):
    mu = _pair_sum(x, ones2) * (1.0 / HEAD_DIM)
    xc = x - mu
    var = _pair_sum(xc * xc, ones2) * (1.0 / HEAD_DIM)
    return xc * lax.rsqrt(var + GN_EPS)


def _block_diag_mask():
    r = lax.broadcasted_iota(jnp.int32, (LANES, LANES), 0)
    c = lax.broadcasted_iota(jnp.int32, (LANES, LANES), 1)
    return (r < HEAD_DIM) == (c < HEAD_DIM)


def _proj_kernel(x_ref, g_ref, w_ref, *o_refs):
    x = x_ref[...]
    rs = lax.rsqrt(jnp.mean(x * x, axis=-1, keepdims=True) + NORM_EPS)
    acc = jnp.dot((x * g_ref[...]).astype(BF16), w_ref[...], preferred_element_type=F32) * rs
    off = 0
    for o_ref in o_refs:
        n = o_ref.shape[1]
        o_ref[...] = acc[:, off:off + n].astype(o_ref.dtype)
        off += n


def _proj(x2, gain, w, outs, name):
    m, d = x2.shape
    n = w.shape[1]
    tm = TM_PROJ
    res = pl.pallas_call(
        _proj_kernel, grid=(m // tm,),
        in_specs=[pl.BlockSpec((tm, d), lambda i: (i, 0)),
                  pl.BlockSpec((1, d), lambda i: (0, 0)),
                  pl.BlockSpec((d, n), lambda i: (0, 0))],
        out_specs=[pl.BlockSpec((tm, wd), lambda i: (i, 0)) for wd, _ in outs],
        out_shape=[jax.ShapeDtypeStruct((m, wd), dt) for wd, dt in outs],
        compiler_params=pltpu.CompilerParams(dimension_semantics=("parallel",),
                                             vmem_limit_bytes=VMEM_LIMIT),
        name=name,
    )(x2, gain, w)
    return res


def _out_kernel(ya_ref, yb_ref, yc_ref, yd_ref, w_ref, g_ref, x_ref, o_ref):
    acc = jnp.dot(ya_ref[...].astype(BF16), w_ref[0], preferred_element_type=F32)
    acc += jnp.dot(yb_ref[...].astype(BF16), w_ref[1], preferred_element_type=F32)
    acc += jnp.dot(yc_ref[...].astype(BF16), w_ref[2], preferred_element_type=F32)
    acc += jnp.dot(yd_ref[...].astype(BF16), w_ref[3], preferred_element_type=F32)
    ms = jnp.mean(acc * acc, axis=-1, keepdims=True)
    o_ref[...] = x_ref[...] + (acc * lax.rsqrt(ms + NORM_EPS)) * g_ref[...]


def _out_proj(ys, w4, gain, x2):
    m, d = x2.shape
    tm = TM_PROJ
    yspec = pl.BlockSpec((tm, G), lambda i: (i, 0))
    return pl.pallas_call(
        _out_kernel, grid=(m // tm,),
        in_specs=[yspec, yspec, yspec, yspec,
                  pl.BlockSpec((4, G, d), lambda i: (0, 0, 0)),
                  pl.BlockSpec((1, d), lambda i: (0, 0)),
                  pl.BlockSpec((tm, d), lambda i: (i, 0))],
        out_specs=pl.BlockSpec((tm, d), lambda i: (i, 0)),
        out_shape=jax.ShapeDtypeStruct((m, d), F32),
        compiler_params=pltpu.CompilerParams(dimension_semantics=("parallel",),
                                             vmem_limit_bytes=VMEM_LIMIT),
        name="out_proj",
    )(*ys, w4, gain, x2)


def _mlstm_kernel(p_ref, g_ref, sh_ref, cw_ref, cb_ref, bcol_ref, nw_ref, o_ref,
                  prev_ref, c_ref, n_ref, m_ref):
    t = p_ref.shape[1]

    @pl.when(pl.program_id(1) == 0)
    def _():
        prev_ref[...] = jnp.zeros_like(prev_ref)
        c_ref[...] = jnp.zeros_like(c_ref)
        n_ref[...] = jnp.zeros_like(n_ref)
        m_ref[...] = jnp.zeros_like(m_ref)

    sb = sh_ref.shape[1]
    nsb = t // sb
    rws = [slice(i * sb, (i + 1) * sb) for i in range(nsb)]
    tail8 = lambda i: p_ref[0, (i + 1) * sb - 2 * SUBLANES:(i + 1) * sb, 0:2 * G].astype(F32)[SUBLANES:]
    qk = [_silu(_causal_conv(p_ref[0, rws[i], 0:2 * G], prev_ref[...] if i == 0 else tail8(i - 1),
                             sh_ref, cw_ref, cb_ref)) for i in range(nsb)]
    prev_ref[...] = tail8(nsb - 1)

    lane = lax.broadcasted_iota(jnp.int32, (sb, LANES), 1)
    m_e = lane < HEAD_DIM
    ones2 = _head_ones()
    ones_row = jnp.ones((2 * sb, LANES), BF16)

    gcol = [g_ref[0, rw, :] + bcol_ref[...] for rw in rws]
    cumc = [_cumsum(_log_sigmoid(gc), 0) for gc in gcol]
    grow = [jnp.where(lane < N_HEADS, gcol[i], cumc[i]).T for i in range(nsb)]

    m_e1 = lax.broadcasted_iota(jnp.int32, (1, LANES), 1) < HEAD_DIM
    tril = (lax.broadcasted_iota(jnp.int32, (sb, sb), 0) >= lax.broadcasted_iota(jnp.int32, (sb, sb), 1))
    bd = _block_diag_mask()

    pairs = range(N_PAIRS)
    sls = [slice(p * LANES, (p + 1) * LANES) for p in pairs]
    cp = [(i, p) for i in range(nsb) for p in pairs]
    ch = [(i, h) for i in range(nsb) for h in range(N_HEADS)]
    ip = lambda i, p: i * N_PAIRS + p
    ih = lambda i, h: i * N_HEADS + h
    sel = lambda xs, i, p: jnp.where(m_e, xs[ih(i, 2 * p)], xs[ih(i, 2 * p + 1)])
    sel1 = lambda xs, i, p: jnp.where(m_e1, xs[ih(i, 2 * p)], xs[ih(i, 2 * p + 1)])
    col = lambda base, p: slice(base + p * LANES, base + (p + 1) * LANES)
    nch = range(len(ch))

    q = [qk[i][:, sls[p]] for i, p in cp]
    k = [qk[i][:, col(G, p)] * (HEAD_DIM ** -0.5) for i, p in cp]
    vb = [p_ref[0, rws[i], col(2 * G, p)] for i, p in cp]
    kb = [x.astype(BF16) for x in k]
    qm = [(jnp.where(m_e, q[ip(i, h // 2)], 0.0) if h % 2 == 0
           else jnp.where(m_e, 0.0, q[ip(i, h // 2)])).astype(BF16) for i, h in ch]
    s_raw = [_dot_nt(qm[c], kb[ip(i, h // 2)]) for c, (i, h) in enumerate(ch)]

    cum_c = [cumc[i][:, N_HEADS + h:N_HEADS + h + 1] for i, h in ch]
    i_c = [gcol[i][:, h:h + 1] for i, h in ch]
    tot = [cum_c[c][sb - 1:sb, :] for c in nch]
    log_w = [tot[c] - cum_c[c] + i_c[c] for c in nch]
    lw_max = [jnp.max(log_w[c], axis=0, keepdims=True) for c in nch]
    m_st = [m_ref[p] for p in pairs]
    m_prev, m_new = [], []
    for c, (i, h) in enumerate(ch):
        mp = (m_st[h // 2][:, (h % 2) * HEAD_DIM:(h % 2) * HEAD_DIM + 1] if i == 0
              else m_new[ih(i - 1, h)])
        m_prev.append(mp)
        m_new.append(jnp.maximum(tot[c] + mp, lw_max[c]))
    log_d = [jnp.where(tril, cum_c[c] - grow[i][N_HEADS + h:N_HEADS + h + 1, :] + grow[i][h:h + 1, :], NEG)
             for c, (i, h) in enumerate(ch)]
    inter = [cum_c[c] + m_prev[c] for c in nch]
    m_t = [jnp.maximum(inter[c], jnp.max(log_d[c], axis=-1, keepdims=True)) for c in nch]
    s = [s_raw[c] * jnp.exp(log_d[c] - m_t[c]) for c in nch]
    s2 = [_split2(s[c]) for c in nch]
    num_h = [jnp.dot(s2[c][:, 0:sb], vb[ip(i, h // 2)], preferred_element_type=F32)
             for c, (i, h) in enumerate(ch)]
    den_h = [jnp.dot(s2[c], ones_row, preferred_element_type=F32) for c in nch]
    s_int_h = [jnp.exp(inter[c] - m_t[c]) for c in nch]
    wj_h = [jnp.exp(log_w[c] - m_new[c]) for c in nch]
    sc_h = [jnp.exp(tot[c] + m_prev[c] - m_new[c]) for c in nch]

    kw = [k[ip(i, p)] * sel(wj_h, i, p) for i, p in cp]
    vk = [_dot_tn(vb[x], kw[x]) for x in range(len(cp))]
    kw_sum = [jnp.sum(kw[x], axis=0, keepdims=True) for x in range(len(cp))]

    c_st = [c_ref[p] for p in pairs]
    n_st = [n_ref[p] for p in pairs]
    q_c, qn = [], []
    for x, (i, p) in enumerate(cp):
        q_c.append(_dot_nt(q[x], c_st[p]))
        qn.append(_pair_sum(q[x] * n_st[p], ones2))
        sc = sel1(sc_h, i, p)
        c_st[p] = sc * c_st[p] + jnp.where(bd, vk[x], 0.0)
        n_st[p] = sc * n_st[p] + kw_sum[x]
    for p in pairs:
        c_ref[p] = c_st[p]
        n_ref[p] = n_st[p]
        m_ref[p] = sel1(m_new, nsb - 1, p)

    for x, (i, p) in enumerate(cp):
        s_int = sel(s_int_h, i, p)
        num = sel(num_h, i, p) + s_int * q_c[x]
        den = sel(den_h, i, p) + s_int * qn[x]
        h_out = num / jnp.maximum(jnp.abs(den), jnp.exp(-sel(m_t, i, p)))
        o_gate = _sigmoid(p_ref[0, rws[i], col(3 * G, p)].astype(F32))
        z = p_ref[0, rws[i], col(4 * G, p)].astype(F32)
        y = _pair_layer_norm(h_out * o_gate, ones2) * nw_ref[:, sls[p]]
        o_ref[0, rws[i], sls[p]] = (y * _silu(z)).astype(o_ref.dtype)


def _mlstm(p_a, gates, conv_w, conv_b, i_bias, f_bias, norm_w):
    b, s, width = p_a.shape
    t = T_MLSTM
    nc = s // t
    bcol = jnp.zeros((1, LANES), F32).at[0, 0:N_HEADS].set(i_bias).at[0, N_HEADS:2 * N_HEADS].set(f_bias)
    const = lambda shape: pl.BlockSpec(shape, lambda bi, ci: (0,) * len(shape))
    return pl.pallas_call(
        _mlstm_kernel, grid=(b, nc),
        in_specs=[pl.BlockSpec((1, t, width), lambda bi, ci: (bi, ci, 0)),
                  pl.BlockSpec((1, t, LANES), lambda bi, ci: (bi, ci, 0)),
                  const((CONV_W - 1, MLSTM_CHUNK, MLSTM_CHUNK)),
                  const((CONV_W, 2 * G)), const((1, 2 * G)), const((1, LANES)), const((1, G))],
        out_specs=pl.BlockSpec((1, t, G), lambda bi, ci: (bi, ci, 0)),
        out_shape=jax.ShapeDtypeStruct((b, s, G), BF16),
        scratch_shapes=[pltpu.VMEM((SUBLANES, 2 * G), F32),
                        pltpu.VMEM((N_PAIRS, LANES, LANES), F32),
                        pltpu.VMEM((N_PAIRS, 1, LANES), F32),
                        pltpu.VMEM((N_PAIRS, 1, LANES), F32)],
        compiler_params=pltpu.CompilerParams(dimension_semantics=("parallel", "arbitrary"),
                                             vmem_limit_bytes=VMEM_LIMIT),
        name="mlstm",
    )(p_a, gates, _shift_matrices(MLSTM_CHUNK), conv_w, conv_b.reshape(1, -1), bcol, norm_w.reshape(1, -1))


def _stack(x, m_e):
    return jnp.concatenate([jnp.where(m_e, x, 0.0), jnp.where(m_e, 0.0, x)], axis=0)


def _rwkv_kernel(p_ref, mu_ref, wlo_ref, b0_ref, kk_ref, ka_ref, rk_ref, gw_ref, gb_ref,
                 msk_ref, o_ref, prev_ref, s_ref):
    t = p_ref.shape[1]
    sh_w = 3 * G + 2 * LORA

    @pl.when(pl.program_id(1) == 0)
    def _():
        prev_ref[...] = jnp.zeros_like(prev_ref)
        s_ref[...] = jnp.zeros_like(s_ref)

    x = p_ref[0, :, 0:sh_w].astype(F32)
    xs = x + mu_ref[...] * (_shift_rows(x, prev_ref[...], 1) - x)
    prev_ref[...] = x[t - SUBLANES:t, :]

    r = xs[:, 0:G]
    k = xs[:, G:2 * G]
    v = xs[:, 2 * G:3 * G]
    lo = xs[:, 3 * G:sh_w]
    lane = lax.broadcasted_iota(jnp.int32, (t, LANES), 1)
    m_e = lane < HEAD_DIM
    lo = jnp.where(m_e, jnp.tanh(lo), lo)
    pre = _dot(lo, wlo_ref[...]) + b0_ref[...]
    log_w = -W_DECAY_SCALE * _sigmoid(pre[:, 0:G])
    a = _sigmoid(pre[:, G:2 * G])

    kappa = k * kk_ref[...]
    k_til = k * (1.0 + (a - 1.0) * ka_ref[...])
    rkk = r * rk_ref[...] * k_til
    kap2 = kappa * kappa

    strict = msk_ref[0] > 0.5
    incl = msk_ref[1] > 0.5
    eye = msk_ref[2]

    ch = RWKV_CHUNK
    n2 = 2 * ch
    m_c = lax.broadcasted_iota(jnp.int32, (ch, LANES), 1) < HEAD_DIM
    pairs = range(N_PAIRS)
    chunks = range(t // ch)
    chains = [(j, p) for j in chunks for p in pairs]
    rows = [slice(j * ch, (j + 1) * ch) for j in chunks]
    sls = [slice(p * LANES, (p + 1) * LANES) for p in pairs]
    blk = lambda arr, c: arr[rows[c[0]], sls[c[1]]]

    log_p = [_cumsum(log_w[rw], 0) for rw in rows]
    p_in = [jnp.exp(lp) for lp in log_p]
    p_inv = [jnp.exp(-lp) for lp in log_p]
    p_prev = [jnp.exp(lp - log_w[rw]) for lp, rw in zip(log_p, rows)]
    p_end = [jnp.exp(lp[ch - 1:ch] - lp) for lp in log_p]
    p_l = [jnp.exp(lp[ch - 1:ch]) for lp in log_p]
    dec = lambda arr, c: arr[c[0]][:, sls[c[1]]]

    kh = [blk(kappa, c) * lax.rsqrt(_pair_sum_lanes(blk(kap2, c), m_c) + 1e-12) for c in chains]
    b_v = [blk(a, c) * kh[i] for i, c in enumerate(chains)]
    stk = lambda val: _stack(val, m_c).astype(BF16)
    khs = [stk(kh[i] * dec(p_prev, c)) for i, c in enumerate(chains)]
    rs = [stk(blk(r, c) * dec(p_in, c)) for c in chains]
    bs = [stk(b_v[i] * dec(p_inv, c)) for i, c in enumerate(chains)]
    kts = [stk(blk(k_til, c) * dec(p_inv, c)) for c in chains]
    vs = [stk(blk(v, c)) for c in chains]
    kte = [stk(blk(k_til, c) * dec(p_end, c)) for c in chains]
    be = [stk(b_v[i] * dec(p_end, c)) for i, c in enumerate(chains)]
    idx = range(len(chains))

    a_rb, av, wu = [], [], []
    for g0 in range(0, len(chains), RWKV_GROUP):
        ids = range(g0, min(g0 + RWKV_GROUP, len(chains)))
        g = {i: _dot_nt(jnp.concatenate([khs[i], rs[i]], axis=0),
                        jnp.concatenate([bs[i], kts[i]], axis=0)) for i in ids}
        a_ub = {i: jnp.where(strict, g[i][0:n2, 0:n2], 0.0) for i in ids}
        a_k = {i: jnp.concatenate([jnp.where(strict, g[i][0:n2, n2:], 0.0),
                                   jnp.where(incl, g[i][n2:, n2:], 0.0)], axis=0).astype(BF16) for i in ids}
        a_rb += [jnp.where(incl, g[i][n2:, 0:n2], 0.0).astype(BF16) for i in ids]
        av_g = {i: _dot(a_k[i], vs[i]) for i in ids}
        av += [av_g[i] for i in ids]

        xinv = {i: eye - a_ub[i] * msk_ref[3] for i in ids}
        for lvl in range(4, 9):
            xa = {i: _dot(xinv[i], a_ub[i] * msk_ref[lvl]) for i in ids}
            xinv = {i: xinv[i] - _dot(xa[i], xinv[i]) for i in ids}
        wu += [_dot(xinv[i], jnp.concatenate([khs[i], av_g[i][0:n2].astype(BF16)], axis=1)) for i in ids]

    bonus_sum = [_pair_sum_lanes(blk(rkk, c), m_c) for c in chains]
    s_st = [s_ref[p] for p in pairs]
    ys = []
    for j in chunks:
        ids = [j * N_PAIRS + p for p in pairs]
        ws = [_dot_nt(jnp.concatenate([wu[i][:, 0:LANES].astype(BF16), rs[i]], axis=0), s_st[p])
              for p, i in enumerate(ids)]
        ub = [(-(ws[p][0:n2] + wu[i][:, LANES:])).astype(BF16) for p, i in enumerate(ids)]
        ys += [ws[p][n2:] + av[i][n2:] + _dot(a_rb[i], ub[p]) for p, i in enumerate(ids)]
        s_st = [s_st[p] * p_l[j][:, sls[p]] + _dot_tn(jnp.concatenate([vs[i], ub[p]], axis=0),
                                                      jnp.concatenate([kte[i], be[i]], axis=0))
                for p, i in enumerate(ids)]
    for p in pairs:
        s_ref[p] = s_st[p]

    wkv = [ys[i][0:ch] + ys[i][ch:n2] for i in idx]
    mu = [_pair_sum_lanes(wkv[i], m_c) * (1.0 / HEAD_DIM) for i in idx]
    xc = [wkv[i] - mu[i] for i in idx]
    var = [_pair_sum_lanes(xc[i] * xc[i], m_c) * (1.0 / HEAD_DIM) for i in idx]
    for i, (j, p) in enumerate(chains):
        z = p_ref[0, rows[j], sh_w + p * LANES:sh_w + (p + 1) * LANES].astype(F32)
        y = (xc[i] * lax.rsqrt(var[i] + GN_EPS)) * gw_ref[:, sls[p]] + gb_ref[:, sls[p]] \
            + bonus_sum[i] * blk(v, chains[i])
        o_ref[0, rows[j], sls[p]] = (y * _silu(z)).astype(o_ref.dtype)


def _rwkv_masks():
    n = 2 * RWKV_CHUNK
    r = np.arange(n)[:, None]
    c = np.arange(n)[None, :]
    ms = [r > c, r >= c, r == c]
    b = 1
    while b < RWKV_CHUNK:
        ms.append((r // (2 * b) == c // (2 * b)) & (r % (2 * b) >= b) & (c % (2 * b) < b))
        b *= 2
    return jnp.asarray(np.stack(ms).astype(np.float32))


def _rwkv(p_b, mu, w_up, w0, a_up, a0, k_k, k_a, r_k, gn_w, gn_b):
    b, s, width = p_b.shape
    t = T_RWKV
    sh_w = 3 * G + 2 * LORA
    w_lo = jnp.zeros((LANES, 2 * G), F32).at[0:LORA, 0:G].set(w_up).at[LORA:, G:].set(a_up).astype(BF16)
    b0 = jnp.concatenate([w0, a0]).reshape(1, 2 * G)
    row = lambda a: a.reshape(1, -1)
    const = lambda shape: pl.BlockSpec(shape, lambda bi, ci: (0,) * len(shape))
    return pl.pallas_call(
        _rwkv_kernel, grid=(b, s // t),
        in_specs=[pl.BlockSpec((1, t, width), lambda bi, ci: (bi, ci, 0)),
                  const((1, sh_w)), const((LANES, 2 * G)), const((1, 2 * G)),
                  const((1, G)), const((1, G)), const((1, G)), const((1, G)), const((1, G)),
                  const((9, 2 * RWKV_CHUNK, 2 * RWKV_CHUNK))],
        out_specs=pl.BlockSpec((1, t, G), lambda bi, ci: (bi, ci, 0)),
        out_shape=jax.ShapeDtypeStruct((b, s, G), BF16),
        scratch_shapes=[pltpu.VMEM((SUBLANES, sh_w), F32),
                        pltpu.VMEM((N_PAIRS, LANES, LANES), F32)],
        compiler_params=pltpu.CompilerParams(dimension_semantics=("parallel", "arbitrary"),
                                             vmem_limit_bytes=VMEM_LIMIT),
        name="rwkv7",
    )(p_b, row(mu), w_lo, b0, row(k_k), row(k_a), row(r_k), row(gn_w), row(gn_b), _rwkv_masks())


def _lru_kernel(p_ref, cw_ref, cb_ref, wr_ref, br_ref, wi_ref, bi_ref, lam_ref, o_ref,
                prev_ref, h_ref):
    t = p_ref.shape[1]

    @pl.when(pl.program_id(1) == 0)
    def _():
        prev_ref[...] = jnp.zeros_like(prev_ref)
        h_ref[...] = jnp.zeros_like(h_ref)

    x = p_ref[0, :, 0:G].astype(F32)
    xc = _causal_conv_rolls(x, prev_ref[...], cw_ref, cb_ref)
    prev_ref[...] = x[t - SUBLANES:t, :]

    xb = xc.astype(BF16)
    r = _sigmoid(jnp.dot(xb, wr_ref[...], preferred_element_type=F32) + br_ref[...])
    i = _sigmoid(jnp.dot(xb, wi_ref[...], preferred_element_type=F32) + bi_ref[...])
    log_a = (-LRU_C * r) * _softplus(-lam_ref[...])
    a = jnp.exp(log_a)
    th = jnp.tanh(log_a)
    u = jnp.sqrt(-2.0 * th / (1.0 - th)) * (i * xc)

    sub = lax.broadcasted_iota(jnp.int32, (SUBLANES, G), 0)
    carry = h_ref[...]
    groups = []
    for i in range(t // SUBLANES):
        rows = slice(i * SUBLANES, (i + 1) * SUBLANES)
        ug, ag = u[rows], a[rows]
        d = 1
        while d < SUBLANES:
            keep = sub >= d
            ug = ug + ag * jnp.where(keep, pltpu.roll(ug, d, axis=0), 0.0)
            ag = ag * jnp.where(keep, pltpu.roll(ag, d, axis=0), 1.0)
            d *= 2
        hg = ug + ag * carry
        carry = hg[SUBLANES - 1:SUBLANES]
        groups.append(hg)
    h_ref[...] = carry
    h = jnp.concatenate(groups, axis=0)
    o_ref[0] = (h * _silu(p_ref[0, :, G:2 * G].astype(F32))).astype(o_ref.dtype)


def _block_diag_weight(w):
    eye = jnp.eye(N_HEADS, dtype=w.dtype)
    return jnp.einsum("hij,hg->higj", w, eye).reshape(G, G)


def _lru(p_c, conv_w, conv_b, w_r, b_r, w_i, b_i, lam):
    b, s, width = p_c.shape
    t = T_LRU
    row = lambda a: a.reshape(1, -1)
    const = lambda shape: pl.BlockSpec(shape, lambda bi, ci: (0,) * len(shape))
    return pl.pallas_call(
        _lru_kernel, grid=(b, s // t),
        in_specs=[pl.BlockSpec((1, t, width), lambda bi, ci: (bi, ci, 0)),
                  const((CONV_W, G)), const((1, G)), const((G, G)), const((1, G)),
                  const((G, G)), const((1, G)), const((1, G))],
        out_specs=pl.BlockSpec((1, t, G), lambda bi, ci: (bi, ci, 0)),
        out_shape=jax.ShapeDtypeStruct((b, s, G), BF16),
        scratch_shapes=[pltpu.VMEM((SUBLANES, G), F32), pltpu.VMEM((1, G), F32)],
        compiler_params=pltpu.CompilerParams(dimension_semantics=("parallel", "arbitrary"),
                                             vmem_limit_bytes=VMEM_LIMIT),
        name="rglru",
    )(p_c, conv_w, row(conv_b), _block_diag_weight(w_r).astype(BF16), row(b_r),
      _block_diag_weight(w_i).astype(BF16), row(b_i), row(lam))


def _ret_kernel(p_ref, cos_ref, sin_ref, dm_ref, xi_ref, zeta_ref, g_ref, nw_ref, o_ref, r_ref):
    t = p_ref.shape[1]

    @pl.when(pl.program_id(1) == 0)
    def _():
        r_ref[...] = jnp.zeros_like(r_ref)

    sb = dm_ref.shape[1]
    lane = lax.broadcasted_iota(jnp.int32, (sb, LANES), 1)
    m_e = lane < HEAD_DIM
    first_half = (lane % HEAD_DIM) < (HEAD_DIM // 2)
    bd = _block_diag_mask()
    ones2 = _head_ones()

    def rope(x, rw):
        swapped = jnp.where(first_half, pltpu.roll(x, LANES - HEAD_DIM // 2, axis=1),
                            pltpu.roll(x, HEAD_DIM // 2, axis=1))
        return x * cos_ref[rw, :] + swapped * sin_ref[rw, :]

    pairs = range(N_PAIRS)
    sls = [slice(p * LANES, (p + 1) * LANES) for p in pairs]
    rws = [slice(i * sb, (i + 1) * sb) for i in range(t // sb)]
    cp = [(rw, p) for rw in rws for p in pairs]
    ch = [(i, half) for i in range(len(cp)) for half in range(2)]
    col = lambda base, p: slice(base + p * LANES, base + (p + 1) * LANES)
    q = [rope(p_ref[0, rw, col(0, p)].astype(F32), rw) for rw, p in cp]
    k = [rope(p_ref[0, rw, col(G, p)].astype(F32), rw) * (HEAD_DIM ** -0.5) for rw, p in cp]
    vb = [p_ref[0, rw, col(2 * G, p)] for rw, p in cp]
    kb = [x.astype(BF16) for x in k]
    qm = [(jnp.where(m_e, q[i], 0.0) if half == 0 else jnp.where(m_e, 0.0, q[i])).astype(BF16)
          for i, half in ch]
    s = [_dot_nt(qm[c], kb[i]) * dm_ref[2 * cp[i][1] + half] for c, (i, half) in enumerate(ch)]
    o_h = [_dot(s[c], vb[i]) for c, (i, _) in enumerate(ch)]
    kv = [_dot_tn(k[i] * zeta_ref[p], vb[i]) for i, (_, p) in enumerate(cp)]
    r_st = [r_ref[p] for p in pairs]
    q_r = []
    for i, (_, p) in enumerate(cp):
        q_r.append(_dot(q[i], r_st[p]))
        r_st[p] = g_ref[p] * r_st[p] + jnp.where(bd, kv[i], 0.0)
    for p in pairs:
        r_ref[p] = r_st[p]
    for i, (rw, p) in enumerate(cp):
        o = jnp.where(m_e, o_h[2 * i], o_h[2 * i + 1]) + xi_ref[p] * q_r[i]
        z = p_ref[0, rw, col(3 * G, p)].astype(F32)
        o_ref[0, rw, sls[p]] = (_pair_layer_norm(o, ones2) * nw_ref[:, sls[p]] * _silu(z)).astype(o_ref.dtype)


def _ret_tables(s):
    t = RET_SB
    half = HEAD_DIM // 2
    pos = jnp.arange(s, dtype=F32)
    inv_freq = ROPE_THETA ** (-jnp.arange(half, dtype=F32) / half)
    ang = pos[:, None] * inv_freq[None, :]
    cos = jnp.tile(jnp.cos(ang), (1, LANES // half))
    sin = jnp.sin(ang)
    sin = jnp.tile(jnp.concatenate([-sin, sin], axis=-1), (1, LANES // HEAD_DIM))
    log_g = jnp.log1p(-jnp.exp2(-5.0 - jnp.arange(N_HEADS, dtype=F32)))
    idx = jnp.arange(t, dtype=F32)
    chunk = jnp.arange(t) // RET_CHUNK
    visible = chunk[:, None] >= chunk[None, :]
    dm = jnp.where(visible[None], jnp.exp(log_g[:, None, None] * jnp.abs(idx[:, None] - idx[None, :])), 0.0)
    pair_lanes = lambda a: jnp.repeat(a.reshape(N_PAIRS, 2, -1), HEAD_DIM, axis=1).transpose(0, 2, 1)
    xi = pair_lanes(jnp.exp(log_g[:, None] * (idx + 1.0)))
    zeta = pair_lanes(jnp.exp(log_g[:, None] * (t - 1.0 - idx)))
    g_blk = pair_lanes(jnp.exp(log_g * t)[:, None])
    return cos, sin, dm, xi, zeta, g_blk


def _retention(p_d, norm_w, tables):
    b, s, width = p_d.shape
    t = T_RET
    cos, sin, dm, xi, zeta, g_blk = tables
    const = lambda shape: pl.BlockSpec(shape, lambda bi, ci: (0,) * len(shape))
    return pl.pallas_call(
        _ret_kernel, grid=(b, s // t),
        in_specs=[pl.BlockSpec((1, t, width), lambda bi, ci: (bi, ci, 0)),
                  pl.BlockSpec((t, LANES), lambda bi, ci: (ci, 0)),
                  pl.BlockSpec((t, LANES), lambda bi, ci: (ci, 0)),
                  const((N_HEADS, RET_SB, RET_SB)), const((N_PAIRS, RET_SB, LANES)),
                  const((N_PAIRS, RET_SB, LANES)), const((N_PAIRS, 1, LANES)), const((1, G))],
        out_specs=pl.BlockSpec((1, t, G), lambda bi, ci: (bi, ci, 0)),
        out_shape=jax.ShapeDtypeStruct((b, s, G), BF16),
        scratch_shapes=[pltpu.VMEM((N_PAIRS, LANES, LANES), F32)],
        compiler_params=pltpu.CompilerParams(dimension_semantics=("parallel", "arbitrary"),
                                             vmem_limit_bytes=VMEM_LIMIT),
        name="retention",
    )(p_d, cos, sin, dm, xi, zeta, g_blk, norm_w.reshape(1, -1))


def kernel(x, norm_pre, norm_post, w_in, w_out, mlstm_conv_w, mlstm_conv_b, mlstm_i_bias, mlstm_f_bias, mlstm_norm_w, rwkv_mu, rwkv_w_up, rwkv_w0, rwkv_a_up, rwkv_a0, rwkv_k_k, rwkv_k_a, rwkv_r_k, rwkv_gn_w, rwkv_gn_b, lru_conv_w, lru_conv_b, lru_w_r, lru_b_r, lru_w_i, lru_b_i, lru_lambda, ret_norm_w):
    b, s, d = x.shape
    depth = w_in.shape[0]
    tables = _ret_tables(s)
    x2 = x.reshape(b * s, d)
    for l in range(depth):
        w = w_in[l]
        gain = norm_pre[l].reshape(1, d)
        w_a = jnp.concatenate([w[:, _A0:_B0], jnp.zeros((d, LANES - 2 * N_HEADS), F32)],
                              axis=1).astype(BF16)
        p_a, gates = _proj(x2, gain, w_a, [(_AG, BF16), (LANES, F32)], "in_proj_mlstm")
        p_b, = _proj(x2, gain, w[:, _B0:_C0].astype(BF16), [(_C0 - _B0, BF16)], "in_proj_rwkv7")
        p_c, = _proj(x2, gain, w[:, _C0:_D0].astype(BF16), [(_D0 - _C0, BF16)], "in_proj_rglru")
        p_d, = _proj(x2, gain, w[:, _D0:_END].astype(BF16), [(_END - _D0, BF16)], "in_proj_retention")

        y_a = _mlstm(p_a.reshape(b, s, -1), gates.reshape(b, s, -1), mlstm_conv_w[l], mlstm_conv_b[l],
                     mlstm_i_bias[l], mlstm_f_bias[l], mlstm_norm_w[l])
        y_b = _rwkv(p_b.reshape(b, s, -1), rwkv_mu[l], rwkv_w_up[l], rwkv_w0[l], rwkv_a_up[l],
                    rwkv_a0[l], rwkv_k_k[l], rwkv_k_a[l], rwkv_r_k[l], rwkv_gn_w[l], rwkv_gn_b[l])
        y_c = _lru(p_c.reshape(b, s, -1), lru_conv_w[l], lru_conv_b[l], lru_w_r[l], lru_b_r[l],
                   lru_w_i[l], lru_b_i[l], lru_lambda[l])
        y_d = _retention(p_d.reshape(b, s, -1), ret_norm_w[l], tables)

        ys = [y.reshape(b * s, G) for y in (y_a, y_b, y_c, y_d)]
        x2 = _out_proj(ys, w_out[l].reshape(4, G, d).astype(BF16), norm_post[l].reshape(1, d), x2)
    return x2.reshape(b, s, d)
```

```python
import functools

import numpy as np
import jax
import jax.numpy as jnp
from jax import lax
from jax.experimental import pallas as pl
from jax.experimental.pallas import tpu as pltpu

F32 = jnp.float32
BF16 = jnp.bfloat16

D_MODEL = 1024
G = 512
N_HEADS = 8
HEAD_DIM = 64
N_PAIRS = N_HEADS // 2
LANES = 128
SUBLANES = 8
CONV_W = 4
LORA = 64
LRU_C = 8.0
W_DECAY_SCALE = 0.606531
ROPE_THETA = 10000.0
NORM_EPS = 1e-6
GN_EPS = 1e-5
RET_CHUNK = 64
NEG = -1e30

_A0, _AG, _B0, _C0, _D0, _END = 0, 2560, 2576, 4752, 5776, 7824

MLSTM_CHUNK = 128
T_MLSTM = 1024
T_RET = 1024
RET_SB = 128
RWKV_CHUNK = 64
T_RWKV = 512
RWKV_GROUP = 16
T_LRU = 1024
TM_PROJ = 1024
VMEM_LIMIT = 48 * 1024 * 1024


def _sigmoid(x):
    return jax.nn.sigmoid(x)


def _silu(x):
    return x * jax.nn.sigmoid(x)


def _softplus(x):
    return jnp.maximum(x, 0.0) + jnp.log1p(jnp.exp(-jnp.abs(x)))


def _log_sigmoid(x):
    return -_softplus(-x)


def _dot(a, b):
    return jnp.dot(a.astype(BF16), b.astype(BF16), preferred_element_type=F32)


def _dot_nt(a, b):
    return lax.dot_general(a.astype(BF16), b.astype(BF16), (((1,), (1,)), ((), ())),
                           preferred_element_type=F32)


def _dot_tn(a, b):
    return lax.dot_general(a.astype(BF16), b.astype(BF16), (((0,), (0,)), ((), ())),
                           preferred_element_type=F32)


def _shift_rows(x, prev8, j):
    xr = pltpu.roll(x, j, axis=0)
    pr = pltpu.roll(prev8, j, axis=0)
    row = lax.broadcasted_iota(jnp.int32, prev8.shape, 0)
    first = jnp.where(row < j, pr, xr[0:SUBLANES])
    return jnp.concatenate([first, xr[SUBLANES:]], axis=0)


def _cumsum(x, axis):
    n = x.shape[axis]
    idx = lax.broadcasted_iota(jnp.int32, x.shape, axis)
    d = 1
    while d < n:
        x = x + jnp.where(idx >= d, pltpu.roll(x, d, axis=axis), 0.0)
        d *= 2
    return x


def _causal_conv(xb, prev8, sh_ref, w_ref, b_ref):
    row8 = lax.broadcasted_iota(jnp.int32, prev8.shape, 0)
    y = xb.astype(F32) * w_ref[CONV_W - 1:CONV_W, :] + b_ref[...]
    for j in range(1, CONV_W):
        xs = jnp.dot(sh_ref[j - 1], xb, preferred_element_type=F32)
        head = xs[0:SUBLANES] + jnp.where(row8 < j, pltpu.roll(prev8, j, axis=0), 0.0)
        xs = jnp.concatenate([head, xs[SUBLANES:]], axis=0)
        y = y + xs * w_ref[CONV_W - 1 - j:CONV_W - j, :]
    return y


def _causal_conv_rolls(x, prev8, w_ref, b_ref):
    y = x * w_ref[CONV_W - 1:CONV_W, :] + b_ref[...]
    for j in range(1, CONV_W):
        y = y + _shift_rows(x, prev8, j) * w_ref[CONV_W - 1 - j:CONV_W - j, :]
    return y


def _shift_matrices(t):
    r = np.arange(t)[:, None]
    c = np.arange(t)[None, :]
    return jnp.asarray(np.stack([(r - c == j) for j in range(1, CONV_W)]).astype(np.float32)).astype(BF16)


def _split2(x):
    hi = x.astype(BF16)
    lo = (x - hi.astype(F32)).astype(BF16)
    return jnp.concatenate([hi, lo], axis=1)


def _head_ones():
    r = lax.broadcasted_iota(jnp.int32, (2 * LANES, LANES), 0) % LANES
    c = lax.broadcasted_iota(jnp.int32, (2 * LANES, LANES), 1)
    return ((r < HEAD_DIM) == (c < HEAD_DIM)).astype(BF16)


def _pair_sum(x, ones2):
    return jnp.dot(_split2(x), ones2, preferred_element_type=F32)


def _pair_sum_lanes(x, m_e):
    se = jnp.sum(jnp.where(m_e, x, 0.0), axis=-1, keepdims=True)
    so = jnp.sum(jnp.where(m_e, 0.0, x), axis=-1, keepdims=True)
    return jnp.where(m_e, se, so)


def _pair_layer_norm(x, ones2):
    mu = _pair_sum(x, ones2) * (1.0 / HEAD_DIM)
    xc = x - mu
    var = _pair_sum(xc * xc, ones2) * (1.0 / HEAD_DIM)
    return xc * lax.rsqrt(var + GN_EPS)


def _block_diag_mask():
    r = lax.broadcasted_iota(jnp.int32, (LANES, LANES), 0)
    c = lax.broadcasted_iota(jnp.int32, (LANES, LANES), 1)
    return (r < HEAD_DIM) == (c < HEAD_DIM)


def _proj_kernel(x_ref, g_ref, w_ref, *o_refs):
    x = x_ref[...]
    rs = lax.rsqrt(jnp.mean(x * x, axis=-1, keepdims=True) + NORM_EPS)
    acc = jnp.dot((x * g_ref[...]).astype(BF16), w_ref[...], preferred_element_type=F32) * rs
    off = 0
    for o_ref in o_refs:
        n = o_ref.shape[1]
        o_ref[...] = acc[:, off:off + n].astype(o_ref.dtype)
        off += n


def _proj(x2, gain, w, outs, name):
    m, d = x2.shape
    n = w.shape[1]
    tm = TM_PROJ
    res = pl.pallas_call(
        _proj_kernel, grid=(m // tm,),
        in_specs=[pl.BlockSpec((tm, d), lambda i: (i, 0)),
                  pl.BlockSpec((1, d), lambda i: (0, 0)),
                  pl.BlockSpec((d, n), lambda i: (0, 0))],
        out_specs=[pl.BlockSpec((tm, wd), lambda i: (i, 0)) for wd, _ in outs],
        out_shape=[jax.ShapeDtypeStruct((m, wd), dt) for wd, dt in outs],
        compiler_params=pltpu.CompilerParams(dimension_semantics=("parallel",),
                                             vmem_limit_bytes=VMEM_LIMIT),
        name=name,
    )(x2, gain, w)
    return res


def _out_kernel(ya_ref, yb_ref, yc_ref, yd_ref, w_ref, g_ref, x_ref, o_ref):
    acc = jnp.dot(ya_ref[...].astype(BF16), w_ref[0], preferred_element_type=F32)
    acc += jnp.dot(yb_ref[...].astype(BF16), w_ref[1], preferred_element_type=F32)
    acc += jnp.dot(yc_ref[...].astype(BF16), w_ref[2], preferred_element_type=F32)
    acc += jnp.dot(yd_ref[...].astype(BF16), w_ref[3], preferred_element_type=F32)
    ms = jnp.mean(acc * acc, axis=-1, keepdims=True)
    o_ref[...] = x_ref[...] + (acc * lax.rsqrt(ms + NORM_EPS)) * g_ref[...]


def _out_proj(ys, w4, gain, x2):
    m, d = x2.shape
    tm = TM_PROJ
    yspec = pl.BlockSpec((tm, G), lambda i: (i, 0))
    return pl.pallas_call(
        _out_kernel, grid=(m // tm,),
        in_specs=[yspec, yspec, yspec, yspec,
                  pl.BlockSpec((4, G, d), lambda i: (0, 0, 0)),
                  pl.BlockSpec((1, d), lambda i: (0, 0)),
                  pl.BlockSpec((tm, d), lambda i: (i, 0))],
        out_specs=pl.BlockSpec((tm, d), lambda i: (i, 0)),
        out_shape=jax.ShapeDtypeStruct((m, d), F32),
        compiler_params=pltpu.CompilerParams(dimension_semantics=("parallel",),
                                             vmem_limit_bytes=VMEM_LIMIT),
        name="out_proj",
    )(*ys, w4, gain, x2)


def _mlstm_kernel(p_ref, g_ref, sh_ref, cw_ref, cb_ref, bcol_ref, nw_ref, o_ref,
                  prev_ref, c_ref, n_ref, m_ref):
    t = p_ref.shape[1]

    @pl.when(pl.program_id(1) == 0)
    def _():
        prev_ref[...] = jnp.zeros_like(prev_ref)
        c_ref[...] = jnp.zeros_like(c_ref)
        n_ref[...] = jnp.zeros_like(n_ref)
        m_ref[...] = jnp.zeros_like(m_ref)

    sb = sh_ref.shape[1]
    nsb = t // sb
    rws = [slice(i * sb, (i + 1) * sb) for i in range(nsb)]
    tail8 = lambda i: p_ref[0, (i + 1) * sb - 2 * SUBLANES:(i + 1) * sb, 0:2 * G].astype(F32)[SUBLANES:]
    qk = [_silu(_causal_conv(p_ref[0, rws[i], 0:2 * G], prev_ref[...] if i == 0 else tail8(i - 1),
                             sh_ref, cw_ref, cb_ref)) for i in range(nsb)]
    prev_ref[...] = tail8(nsb - 1)

    lane = lax.broadcasted_iota(jnp.int32, (sb, LANES), 1)
    m_e = lane < HEAD_DIM
    ones2 = _head_ones()
    ones_row = jnp.ones((2 * sb, LANES), BF16)

    gcol = [g_ref[0, rw, :] + bcol_ref[...] for rw in rws]
    cumc = [_cumsum(_log_sigmoid(gc), 0) for gc in gcol]
    grow = [jnp.where(lane < N_HEADS, gcol[i], cumc[i]).T for i in range(nsb)]

    m_e1 = lax.broadcasted_iota(jnp.int32, (1, LANES), 1) < HEAD_DIM
    tril = (lax.broadcasted_iota(jnp.int32, (sb, sb), 0) >= lax.broadcasted_iota(jnp.int32, (sb, sb), 1))
    bd = _block_diag_mask()

    pairs = range(N_PAIRS)
    sls = [slice(p * LANES, (p + 1) * LANES) for p in pairs]
    cp = [(i, p) for i in range(nsb) for p in pairs]
    ch = [(i, h) for i in range(nsb) for h in range(N_HEADS)]
    ip = lambda i, p: i * N_PAIRS + p
    ih = lambda i, h: i * N_HEADS + h
    sel = lambda xs, i, p: jnp.where(m_e, xs[ih(i, 2 * p)], xs[ih(i, 2 * p + 1)])
    sel1 = lambda xs, i, p: jnp.where(m_e1, xs[ih(i, 2 * p)], xs[ih(i, 2 * p + 1)])
    col = lambda base, p: slice(base + p * LANES, base + (p + 1) * LANES)
    nch = range(len(ch))

    q = [qk[i][:, sls[p]] for i, p in cp]
    k = [qk[i][:, col(G, p)] * (HEAD_DIM ** -0.5) for i, p in cp]
    vb = [p_ref[0, rws[i], col(2 * G, p)] for i, p in cp]
    kb = [x.astype(BF16) for x in k]
    qm = [(jnp.where(m_e, q[ip(i, h // 2)], 0.0) if h % 2 == 0
           else jnp.where(m_e, 0.0, q[ip(i, h // 2)])).astype(BF16) for i, h in ch]
    s_raw = [_dot_nt(qm[c], kb[ip(i, h // 2)]) for c, (i, h) in enumerate(ch)]

    cum_c = [cumc[i][:, N_HEADS + h:N_HEADS + h + 1] for i, h in ch]
    i_c = [gcol[i][:, h:h + 1] for i, h in ch]
    tot = [cum_c[c][sb - 1:sb, :] for c in nch]
    log_w = [tot[c] - cum_c[c] + i_c[c] for c in nch]
    lw_max = [jnp.max(log_w[c], axis=0, keepdims=True) for c in nch]
    m_st = [m_ref[p] for p in pairs]
    m_prev, m_new = [], []
    for c, (i, h) in enumerate(ch):
        mp = (m_st[h // 2][:, (h % 2) * HEAD_DIM:(h % 2) * HEAD_DIM + 1] if i == 0
              else m_new[ih(i - 1, h)])
        m_prev.append(mp)
        m_new.append(jnp.maximum(tot[c] + mp, lw_max[c]))
    log_d = [jnp.where(tril, cum_c[c] - grow[i][N_HEADS + h:N_HEADS + h + 1, :] + grow[i][h:h + 1, :], NEG)
             for c, (i, h) in enumerate(ch)]
    inter = [cum_c[c] + m_prev[c] for c in nch]
    m_t = [jnp.maximum(inter[c], jnp.max(log_d[c], axis=-1, keepdims=True)) for c in nch]
    s = [s_raw[c] * jnp.exp(log_d[c] - m_t[c]) for c in nch]
    s2 = [_split2(s[c]) for c in nch]
    num_h = [jnp.dot(s2[c][:, 0:sb], vb[ip(i, h // 2)], preferred_element_type=F32)
             for c, (i, h) in enumerate(ch)]
    den_h = [jnp.dot(s2[c], ones_row, preferred_element_type=F32) for c in nch]
    s_int_h = [jnp.exp(inter[c] - m_t[c]) for c in nch]
    wj_h = [jnp.exp(log_w[c] - m_new[c]) for c in nch]
    sc_h = [jnp.exp(tot[c] + m_prev[c] - m_new[c]) for c in nch]

    kw = [k[ip(i, p)] * sel(wj_h, i, p) for i, p in cp]
    vk = [_dot_tn(vb[x], kw[x]) for x in range(len(cp))]
    kw_sum = [jnp.sum(kw[x], axis=0, keepdims=True) for x in range(len(cp))]

    c_st = [c_ref[p] for p in pairs]
    n_st = [n_ref[p] for p in pairs]
    q_c, qn = [], []
    for x, (i, p) in enumerate(cp):
        q_c.append(_dot_nt(q[x], c_st[p]))
        qn.append(_pair_sum(q[x] * n_st[p], ones2))
        sc = sel1(sc_h, i, p)
        c_st[p] = sc * c_st[p] + jnp.where(bd, vk[x], 0.0)
        n_st[p] = sc * n_st[p] + kw_sum[x]
    for p in pairs:
        c_ref[p] = c_st[p]
        n_ref[p] = n_st[p]
        m_ref[p] = sel1(m_new, nsb - 1, p)

    for x, (i, p) in enumerate(cp):
        s_int = sel(s_int_h, i, p)
        num = sel(num_h, i, p) + s_int * q_c[x]
        den = sel(den_h, i, p) + s_int * qn[x]
        h_out = num / jnp.maximum(jnp.abs(den), jnp.exp(-sel(m_t, i, p)))
        o_gate = _sigmoid(p_ref[0, rws[i], col(3 * G, p)].astype(F32))
        z = p_ref[0, rws[i], col(4 * G, p)].astype(F32)
        y = _pair_layer_norm(h_out * o_gate, ones2) * nw_ref[:, sls[p]]
        o_ref[0, rws[i], sls[p]] = (y * _silu(z)).astype(o_ref.dtype)


def _mlstm(p_a, gates, conv_w, conv_b, i_bias, f_bias, norm_w):
    b, s, width = p_a.shape
    t = T_MLSTM
    nc = s // t
    bcol = jnp.zeros((1, LANES), F32).at[0, 0:N_HEADS].set(i_bias).at[0, N_HEADS:2 * N_HEADS].set(f_bias)
    const = lambda shape: pl.BlockSpec(shape, lambda bi, ci: (0,) * len(shape))
    return pl.pallas_call(
        _mlstm_kernel, grid=(b, nc),
        in_specs=[pl.BlockSpec((1, t, width), lambda bi, ci: (bi, ci, 0)),
                  pl.BlockSpec((1, t, LANES), lambda bi, ci: (bi, ci, 0)),
                  const((CONV_W - 1, MLSTM_CHUNK, MLSTM_CHUNK)),
                  const((CONV_W, 2 * G)), const((1, 2 * G)), const((1, LANES)), const((1, G))],
        out_specs=pl.BlockSpec((1, t, G), lambda bi, ci: (bi, ci, 0)),
        out_shape=jax.ShapeDtypeStruct((b, s, G), BF16),
        scratch_shapes=[pltpu.VMEM((SUBLANES, 2 * G), F32),
                        pltpu.VMEM((N_PAIRS, LANES, LANES), F32),
                        pltpu.VMEM((N_PAIRS, 1, LANES), F32),
                        pltpu.VMEM((N_PAIRS, 1, LANES), F32)],
        compiler_params=pltpu.CompilerParams(dimension_semantics=("parallel", "arbitrary"),
                                             vmem_limit_bytes=VMEM_LIMIT),
        name="mlstm",
    )(p_a, gates, _shift_matrices(MLSTM_CHUNK), conv_w, conv_b.reshape(1, -1), bcol, norm_w.reshape(1, -1))


def _stack(x, m_e):
    return jnp.concatenate([jnp.where(m_e, x, 0.0), jnp.where(m_e, 0.0, x)], axis=0)


def _rwkv_kernel(p_ref, mu_ref, wlo_ref, b0_ref, kk_ref, ka_ref, rk_ref, gw_ref, gb_ref,
                 msk_ref, o_ref, prev_ref, s_ref):
    t = p_ref.shape[1]
    sh_w = 3 * G + 2 * LORA

    @pl.when(pl.program_id(1) == 0)
    def _():
        prev_ref[...] = jnp.zeros_like(prev_ref)
        s_ref[...] = jnp.zeros_like(s_ref)

    x = p_ref[0, :, 0:sh_w].astype(F32)
    xs = x + mu_ref[...] * (_shift_rows(x, prev_ref[...], 1) - x)
    prev_ref[...] = x[t - SUBLANES:t, :]

    r = xs[:, 0:G]
    k = xs[:, G:2 * G]
    v = xs[:, 2 * G:3 * G]
    lo = xs[:, 3 * G:sh_w]
    lane = lax.broadcasted_iota(jnp.int32, (t, LANES), 1)
    m_e = lane < HEAD_DIM
    lo = jnp.where(m_e, jnp.tanh(lo), lo)
    pre = _dot(lo, wlo_ref[...]) + b0_ref[...]
    log_w = -W_DECAY_SCALE * _sigmoid(pre[:, 0:G])
    a = _sigmoid(pre[:, G:2 * G])

    kappa = k * kk_ref[...]
    k_til = k * (1.0 + (a - 1.0) * ka_ref[...])
    rkk = r * rk_ref[...] * k_til
    kap2 = kappa * kappa

    strict = msk_ref[0] > 0.5
    incl = msk_ref[1] > 0.5
    eye = msk_ref[2]

    ch = RWKV_CHUNK
    n2 = 2 * ch
    m_c = lax.broadcasted_iota(jnp.int32, (ch, LANES), 1) < HEAD_DIM
    pairs = range(N_PAIRS)
    chunks = range(t // ch)
    chains = [(j, p) for j in chunks for p in pairs]
    rows = [slice(j * ch, (j + 1) * ch) for j in chunks]
    sls = [slice(p * LANES, (p + 1) * LANES) for p in pairs]
    blk = lambda arr, c: arr[rows[c[0]], sls[c[1]]]

    log_p = [_cumsum(log_w[rw], 0) for rw in rows]
    p_in = [jnp.exp(lp) for lp in log_p]
    p_inv = [jnp.exp(-lp) for lp in log_p]
    p_prev = [jnp.exp(lp - log_w[rw]) for lp, rw in zip(log_p, rows)]
    p_end = [jnp.exp(lp[ch - 1:ch] - lp) for lp in log_p]
    p_l = [jnp.exp(lp[ch - 1:ch]) for lp in log_p]
    dec = lambda arr, c: arr[c[0]][:, sls[c[1]]]

    kh = [blk(kappa, c) * lax.rsqrt(_pair_sum_lanes(blk(kap2, c), m_c) + 1e-12) for c in chains]
    b_v = [blk(a, c) * kh[i] for i, c in enumerate(chains)]
    stk = lambda val: _stack(val, m_c).astype(BF16)
    khs = [stk(kh[i] * dec(p_prev, c)) for i, c in enumerate(chains)]
    rs = [stk(blk(r, c) * dec(p_in, c)) for c in chains]
    bs = [stk(b_v[i] * dec(p_inv, c)) for i, c in enumerate(chains)]
    kts = [stk(blk(k_til, c) * dec(p_inv, c)) for c in chains]
    vs = [stk(blk(v, c)) for c in chains]
    kte = [stk(blk(k_til, c) * dec(p_end, c)) for c in chains]
    be = [stk(b_v[i] * dec(p_end, c)) for i, c in enumerate(chains)]
    idx = range(len(chains))

    a_rb, av, wu = [], [], []
    for g0 in range(0, len(chains), RWKV_GROUP):
        ids = range(g0, min(g0 + RWKV_GROUP, len(chains)))
        g = {i: _dot_nt(jnp.concatenate([khs[i], rs[i]], axis=0),
                        jnp.concatenate([bs[i], kts[i]], axis=0)) for i in ids}
        a_ub = {i: jnp.where(strict, g[i][0:n2, 0:n2], 0.0) for i in ids}
        a_k = {i: jnp.concatenate([jnp.where(strict, g[i][0:n2, n2:], 0.0),
                                   jnp.where(incl, g[i][n2:, n2:], 0.0)], axis=0).astype(BF16) for i in ids}
        a_rb += [jnp.where(incl, g[i][n2:, 0:n2], 0.0).astype(BF16) for i in ids]
        av_g = {i: _dot(a_k[i], vs[i]) for i in ids}
        av += [av_g[i] for i in ids]

        xinv = {i: eye - a_ub[i] * msk_ref[3] for i in ids}
        for lvl in range(4, 9):
            xa = {i: _dot(xinv[i], a_ub[i] * msk_ref[lvl]) for i in ids}
            xinv = {i: xinv[i] - _dot(xa[i], xinv[i]) for i in ids}
        wu += [_dot(xinv[i], jnp.concatenate([khs[i], av_g[i][0:n2].astype(BF16)], axis=1)) for i in ids]

    bonus_sum = [_pair_sum_lanes(blk(rkk, c), m_c) for c in chains]
    s_st = [s_ref[p] for p in pairs]
    ys = []
    for j in chunks:
        ids = [j * N_PAIRS + p for p in pairs]
        ws = [_dot_nt(jnp.concatenate([wu[i][:, 0:LANES].astype(BF16), rs[i]], axis=0), s_st[p])
              for p, i in enumerate(ids)]
        ub = [(-(ws[p][0:n2] + wu[i][:, LANES:])).astype(BF16) for p, i in enumerate(ids)]
        ys += [ws[p][n2:] + av[i][n2:] + _dot(a_rb[i], ub[p]) for p, i in enumerate(ids)]
        s_st = [s_st[p] * p_l[j][:, sls[p]] + _dot_tn(jnp.concatenate([vs[i], ub[p]], axis=0),
                                                      jnp.concatenate([kte[i], be[i]], axis=0))
                for p, i in enumerate(ids)]
    for p in pairs:
        s_ref[p] = s_st[p]

    wkv = [ys[i][0:ch] + ys[i][ch:n2] for i in idx]
    mu = [_pair_sum_lanes(wkv[i], m_c) * (1.0 / HEAD_DIM) for i in idx]
    xc = [wkv[i] - mu[i] for i in idx]
    var = [_pair_sum_lanes(xc[i] * xc[i], m_c) * (1.0 / HEAD_DIM) for i in idx]
    for i, (j, p) in enumerate(chains):
        z = p_ref[0, rows[j], sh_w + p * LANES:sh_w + (p + 1) * LANES].astype(F32)
        y = (xc[i] * lax.rsqrt(var[i] + GN_EPS)) * gw_ref[:, sls[p]] + gb_ref[:, sls[p]] \
            + bonus_sum[i] * blk(v, chains[i])
        o_ref[0, rows[j], sls[p]] = (y * _silu(z)).astype(o_ref.dtype)


def _rwkv_masks():
    n = 2 * RWKV_CHUNK
    r = np.arange(n)[:, None]
    c = np.arange(n)[None, :]
    ms = [r > c, r >= c, r == c]
    b = 1
    while b < RWKV_CHUNK:
        ms.append((r // (2 * b) == c // (2 * b)) & (r % (2 * b) >= b) & (c % (2 * b) < b))
        b *= 2
    return jnp.asarray(np.stack(ms).astype(np.float32))


def _rwkv(p_b, mu, w_up, w0, a_up, a0, k_k, k_a, r_k, gn_w, gn_b):
    b, s, width = p_b.shape
    t = T_RWKV
    sh_w = 3 * G + 2 * LORA
    w_lo = jnp.zeros((LANES, 2 * G), F32).at[0:LORA, 0:G].set(w_up).at[LORA:, G:].set(a_up).astype(BF16)
    b0 = jnp.concatenate([w0, a0]).reshape(1, 2 * G)
    row = lambda a: a.reshape(1, -1)
    const = lambda shape: pl.BlockSpec(shape, lambda bi, ci: (0,) * len(shape))
    return pl.pallas_call(
        _rwkv_kernel, grid=(b, s // t),
        in_specs=[pl.BlockSpec((1, t, width), lambda bi, ci: (bi, ci, 0)),
                  const((1, sh_w)), const((LANES, 2 * G)), const((1, 2 * G)),
                  const((1, G)), const((1, G)), const((1, G)), const((1, G)), const((1, G)),
                  const((9, 2 * RWKV_CHUNK, 2 * RWKV_CHUNK))],
        out_specs=pl.BlockSpec((1, t, G), lambda bi, ci: (bi, ci, 0)),
        out_shape=jax.ShapeDtypeStruct((b, s, G), BF16),
        scratch_shapes=[pltpu.VMEM((SUBLANES, sh_w), F32),
                        pltpu.VMEM((N_PAIRS, LANES, LANES), F32)],
        compiler_params=pltpu.CompilerParams(dimension_semantics=("parallel", "arbitrary"),
                                             vmem_limit_bytes=VMEM_LIMIT),
        name="rwkv7",
    )(p_b, row(mu), w_lo, b0, row(k_k), row(k_a), row(r_k), row(gn_w), row(gn_b), _rwkv_masks())


def _lru_kernel(p_ref, cw_ref, cb_ref, wr_ref, br_ref, wi_ref, bi_ref, lam_ref, o_ref,
                prev_ref, h_ref):
    t = p_ref.shape[1]

    @pl.when(pl.program_id(1) == 0)
    def _():
        prev_ref[...] = jnp.zeros_like(prev_ref)
        h_ref[...] = jnp.zeros_like(h_ref)

    x = p_ref[0, :, 0:G].astype(F32)
    xc = _causal_conv_rolls(x, prev_ref[...], cw_ref, cb_ref)
    prev_ref[...] = x[t - SUBLANES:t, :]

    xb = xc.astype(BF16)
    r = _sigmoid(jnp.dot(xb, wr_ref[...], preferred_element_type=F32) + br_ref[...])
    i = _sigmoid(jnp.dot(xb, wi_ref[...], preferred_element_type=F32) + bi_ref[...])
    log_a = (-LRU_C * r) * _softplus(-lam_ref[...])
    a = jnp.exp(log_a)
    th = jnp.tanh(log_a)
    u = jnp.sqrt(-2.0 * th / (1.0 - th)) * (i * xc)

    sub = lax.broadcasted_iota(jnp.int32, (SUBLANES, G), 0)
    carry = h_ref[...]
    groups = []
    for i in range(t // SUBLANES):
        rows = slice(i * SUBLANES, (i + 1) * SUBLANES)
        ug, ag = u[rows], a[rows]
        d = 1
        while d < SUBLANES:
            keep = sub >= d
            ug = ug + ag * jnp.where(keep, pltpu.roll(ug, d, axis=0), 0.0)
            ag = ag * jnp.where(keep, pltpu.roll(ag, d, axis=0), 1.0)
            d *= 2
        hg = ug + ag * carry
        carry = hg[SUBLANES - 1:SUBLANES]
        groups.append(hg)
    h_ref[...] = carry
    h = jnp.concatenate(groups, axis=0)
    o_ref[0] = (h * _silu(p_ref[0, :, G:2 * G].astype(F32))).astype(o_ref.dtype)


def _block_diag_weight(w):
    eye = jnp.eye(N_HEADS, dtype=w.dtype)
    return jnp.einsum("hij,hg->higj", w, eye).reshape(G, G)


def _lru(p_c, conv_w, conv_b, w_r, b_r, w_i, b_i, lam):
    b, s, width = p_c.shape
    t = T_LRU
    row = lambda a: a.reshape(1, -1)
    const = lambda shape: pl.BlockSpec(shape, lambda bi, ci: (0,) * len(shape))
    return pl.pallas_call(
        _lru_kernel, grid=(b, s // t),
        in_specs=[pl.BlockSpec((1, t, width), lambda bi, ci: (bi, ci, 0)),
                  const((CONV_W, G)), const((1, G)), const((G, G)), const((1, G)),
                  const((G, G)), const((1, G)), const((1, G))],
        out_specs=pl.BlockSpec((1, t, G), lambda bi, ci: (bi, ci, 0)),
        out_shape=jax.ShapeDtypeStruct((b, s, G), BF16),
        scratch_shapes=[pltpu.VMEM((SUBLANES, G), F32), pltpu.VMEM((1, G), F32)],
        compiler_params=pltpu.CompilerParams(dimension_semantics=("parallel", "arbitrary"),
                                             vmem_limit_bytes=VMEM_LIMIT),
        name="rglru",
    )(p_c, conv_w, row(conv_b), _block_diag_weight(w_r).astype(BF16), row(b_r),
      _block_diag_weight(w_i).astype(BF16), row(b_i), row(lam))


def _ret_kernel(p_ref, cos_ref, sin_ref, dm_ref, xi_ref, zeta_ref, g_ref, nw_ref, o_ref, r_ref):
    t = p_ref.shape[1]

    @pl.when(pl.program_id(1) == 0)
    def _():
        r_ref[...] = jnp.zeros_like(r_ref)

    sb = dm_ref.shape[1]
    lane = lax.broadcasted_iota(jnp.int32, (sb, LANES), 1)
    m_e = lane < HEAD_DIM
    first_half = (lane % HEAD_DIM) < (HEAD_DIM // 2)
    bd = _block_diag_mask()
    ones2 = _head_ones()

    def rope(x, rw):
        swapped = jnp.where(first_half, pltpu.roll(x, LANES - HEAD_DIM // 2, axis=1),
                            pltpu.roll(x, HEAD_DIM // 2, axis=1))
        return x * cos_ref[rw, :] + swapped * sin_ref[rw, :]

    pairs = range(N_PAIRS)
    sls = [slice(p * LANES, (p + 1) * LANES) for p in pairs]
    rws = [slice(i * sb, (i + 1) * sb) for i in range(t // sb)]
    cp = [(rw, p) for rw in rws for p in pairs]
    ch = [(i, half) for i in range(len(cp)) for half in range(2)]
    col = lambda base, p: slice(base + p * LANES, base + (p + 1) * LANES)
    q = [rope(p_ref[0, rw, col(0, p)].astype(F32), rw) for rw, p in cp]
    k = [rope(p_ref[0, rw, col(G, p)].astype(F32), rw) * (HEAD_DIM ** -0.5) for rw, p in cp]
    vb = [p_ref[0, rw, col(2 * G, p)] for rw, p in cp]
    kb = [x.astype(BF16) for x in k]
    qm = [(jnp.where(m_e, q[i], 0.0) if half == 0 else jnp.where(m_e, 0.0, q[i])).astype(BF16)
          for i, half in ch]
    s = [_dot_nt(qm[c], kb[i]) * dm_ref[2 * cp[i][1] + half] for c, (i, half) in enumerate(ch)]
    o_h = [_dot(s[c], vb[i]) for c, (i, _) in enumerate(ch)]
    kv = [_dot_tn(k[i] * zeta_ref[p], vb[i]) for i, (_, p) in enumerate(cp)]
    r_st = [r_ref[p] for p in pairs]
    q_r = []
    for i, (_, p) in enumerate(cp):
        q_r.append(_dot(q[i], r_st[p]))
        r_st[p] = g_ref[p] * r_st[p] + jnp.where(bd, kv[i], 0.0)
    for p in pairs:
        r_ref[p] = r_st[p]
    for i, (rw, p) in enumerate(cp):
        o = jnp.where(m_e, o_h[2 * i], o_h[2 * i + 1]) + xi_ref[p] * q_r[i]
        z = p_ref[0, rw, col(3 * G, p)].astype(F32)
        o_ref[0, rw, sls[p]] = (_pair_layer_norm(o, ones2) * nw_ref[:, sls[p]] * _silu(z)).astype(o_ref.dtype)


def _ret_tables(s):
    t = RET_SB
    half = HEAD_DIM // 2
    pos = jnp.arange(s, dtype=F32)
    inv_freq = ROPE_THETA ** (-jnp.arange(half, dtype=F32) / half)
    ang = pos[:, None] * inv_freq[None, :]
    cos = jnp.tile(jnp.cos(ang), (1, LANES // half))
    sin = jnp.sin(ang)
    sin = jnp.tile(jnp.concatenate([-sin, sin], axis=-1), (1, LANES // HEAD_DIM))
    log_g = jnp.log1p(-jnp.exp2(-5.0 - jnp.arange(N_HEADS, dtype=F32)))
    idx = jnp.arange(t, dtype=F32)
    chunk = jnp.arange(t) // RET_CHUNK
    visible = chunk[:, None] >= chunk[None, :]
    dm = jnp.where(visible[None], jnp.exp(log_g[:, None, None] * jnp.abs(idx[:, None] - idx[None, :])), 0.0)
    pair_lanes = lambda a: jnp.repeat(a.reshape(N_PAIRS, 2, -1), HEAD_DIM, axis=1).transpose(0, 2, 1)
    xi = pair_lanes(jnp.exp(log_g[:, None] * (idx + 1.0)))
    zeta = pair_lanes(jnp.exp(log_g[:, None] * (t - 1.0 - idx)))
    g_blk = pair_lanes(jnp.exp(log_g * t)[:, None])
    return cos, sin, dm, xi, zeta, g_blk


def _retention(p_d, norm_w, tables):
    b, s, width = p_d.shape
    t = T_RET
    cos, sin, dm, xi, zeta, g_blk = tables
    const = lambda shape: pl.BlockSpec(shape, lambda bi, ci: (0,) * len(shape))
    return pl.pallas_call(
        _ret_kernel, grid=(b, s // t),
        in_specs=[pl.BlockSpec((1, t, width), lambda bi, ci: (bi, ci, 0)),
                  pl.BlockSpec((t, LANES), lambda bi, ci: (ci, 0)),
                  pl.BlockSpec((t, LANES), lambda bi, ci: (ci, 0)),
                  const((N_HEADS, RET_SB, RET_SB)), const((N_PAIRS, RET_SB, LANES)),
                  const((N_PAIRS, RET_SB, LANES)), const((N_PAIRS, 1, LANES)), const((1, G))],
        out_specs=pl.BlockSpec((1, t, G), lambda bi, ci: (bi, ci, 0)),
        out_shape=jax.ShapeDtypeStruct((b, s, G), BF16),
        scratch_shapes=[pltpu.VMEM((N_PAIRS, LANES, LANES), F32)],
        compiler_params=pltpu.CompilerParams(dimension_semantics=("parallel", "arbitrary"),
                                             vmem_limit_bytes=VMEM_LIMIT),
        name="retention",
    )(p_d, cos, sin, dm, xi, zeta, g_blk, norm_w.reshape(1, -1))


def kernel(x, norm_pre, norm_post, w_in, w_out, mlstm_conv_w, mlstm_conv_b, mlstm_i_bias, mlstm_f_bias, mlstm_norm_w, rwkv_mu, rwkv_w_up, rwkv_w0, rwkv_a_up, rwkv_a0, rwkv_k_k, rwkv_k_a, rwkv_r_k, rwkv_gn_w, rwkv_gn_b, lru_conv_w, lru_conv_b, lru_w_r, lru_b_r, lru_w_i, lru_b_i, lru_lambda, ret_norm_w):
    b, s, d = x.shape
    depth = w_in.shape[0]
    tables = _ret_tables(s)
    x2 = x.reshape(b * s, d)
    for l in range(depth):
        w = w_in[l]
        gain = norm_pre[l].reshape(1, d)
        w_a = jnp.concatenate([w[:, _A0:_B0], jnp.zeros((d, LANES - 2 * N_HEADS), F32)],
                              axis=1).astype(BF16)
        p_a, gates = _proj(x2, gain, w_a, [(_AG, BF16), (LANES, F32)], "in_proj_mlstm")
        p_b, = _proj(x2, gain, w[:, _B0:_C0].astype(BF16), [(_C0 - _B0, BF16)], "in_proj_rwkv7")
        p_c, = _proj(x2, gain, w[:, _C0:_D0].astype(BF16), [(_D0 - _C0, BF16)], "in_proj_rglru")
        p_d, = _proj(x2, gain, w[:, _D0:_END].astype(BF16), [(_END - _D0, BF16)], "in_proj_retention")

        y_a = _mlstm(p_a.reshape(b, s, -1), gates.reshape(b, s, -1), mlstm_conv_w[l], mlstm_conv_b[l],
                     mlstm_i_bias[l], mlstm_f_bias[l], mlstm_norm_w[l])
        y_b = _rwkv(p_b.reshape(b, s, -1), rwkv_mu[l], rwkv_w_up[l], rwkv_w0[l], rwkv_a_up[l],
                    rwkv_a0[l], rwkv_k_k[l], rwkv_k_a[l], rwkv_r_k[l], rwkv_gn_w[l], rwkv_gn_b[l])
        y_c = _lru(p_c.reshape(b, s, -1), lru_conv_w[l], lru_conv_b[l], lru_w_r[l], lru_b_r[l],
                   lru_w_i[l], lru_b_i[l], lru_lambda[l])
        y_d = _retention(p_d.reshape(b, s, -1), ret_norm_w[l], tables)

        ys = [y.reshape(b * s, G) for y in (y_a, y_b, y_c, y_d)]
        x2 = _out_proj(ys, w_out[l].reshape(4, G, d).astype(BF16), norm_post[l].reshape(1, d), x2)
    return x2.reshape(b, s, d)
```

```python
import functools

import numpy as np
import jax
import jax.numpy as jnp
from jax import lax
from jax.experimental import pallas as pl
from jax.experimental.pallas import tpu as pltpu

F32 = jnp.float32
BF16 = jnp.bfloat16

D_MODEL = 1024
G = 512
N_HEADS = 8
HEAD_DIM = 64
N_PAIRS = N_HEADS // 2
LANES = 128
SUBLANES = 8
CONV_W = 4
LORA = 64
LRU_C = 8.0
W_DECAY_SCALE = 0.606531
ROPE_THETA = 10000.0
NORM_EPS = 1e-6
GN_EPS = 1e-5
RET_CHUNK = 64
NEG = -1e30

_A0, _AG, _B0, _C0, _D0, _END = 0, 2560, 2576, 4752, 5776, 7824

MLSTM_CHUNK = 128
T_MLSTM = 512
T_RET = 512
RET_SB = 128
RWKV_CHUNK = 64
T_RWKV = 512
RWKV_GROUP = 16
T_LRU = 1024
TM_PROJ = 1024
VMEM_LIMIT = 48 * 1024 * 1024


def _sigmoid(x):
    return jax.nn.sigmoid(x)


def _silu(x):
    return x * jax.nn.sigmoid(x)


def _softplus(x):
    return jnp.maximum(x, 0.0) + jnp.log1p(jnp.exp(-jnp.abs(x)))


def _log_sigmoid(x):
    return -_softplus(-x)


def _dot(a, b):
    return jnp.dot(a.astype(BF16), b.astype(BF16), preferred_element_type=F32)


def _dot_nt(a, b):
    return lax.dot_general(a.astype(BF16), b.astype(BF16), (((1,), (1,)), ((), ())),
                           preferred_element_type=F32)


def _dot_tn(a, b):
    return lax.dot_general(a.astype(BF16), b.astype(BF16), (((0,), (0,)), ((), ())),
                           preferred_element_type=F32)


def _shift_rows(x, prev8, j):
    xr = pltpu.roll(x, j, axis=0)
    pr = pltpu.roll(prev8, j, axis=0)
    row = lax.broadcasted_iota(jnp.int32, prev8.shape, 0)
    first = jnp.where(row < j, pr, xr[0:SUBLANES])
    return jnp.concatenate([first, xr[SUBLANES:]], axis=0)


def _cumsum(x, axis):
    n = x.shape[axis]
    idx = lax.broadcasted_iota(jnp.int32, x.shape, axis)
    d = 1
    while d < n:
        x = x + jnp.where(idx >= d, pltpu.roll(x, d, axis=axis), 0.0)
        d *= 2
    return x


def _causal_conv(xb, prev8, sh_ref, w_ref, b_ref):
    row8 = lax.broadcasted_iota(jnp.int32, prev8.shape, 0)
    y = xb.astype(F32) * w_ref[CONV_W - 1:CONV_W, :] + b_ref[...]
    for j in range(1, CONV_W):
        xs = jnp.dot(sh_ref[j - 1], xb, preferred_element_type=F32)
        head = xs[0:SUBLANES] + jnp.where(row8 < j, pltpu.roll(prev8, j, axis=0), 0.0)
        xs = jnp.concatenate([head, xs[SUBLANES:]], axis=0)
        y = y + xs * w_ref[CONV_W - 1 - j:CONV_W - j, :]
    return y


def _causal_conv_rolls(x, prev8, w_ref, b_ref):
    y = x * w_ref[CONV_W - 1:CONV_W, :] + b_ref[...]
    for j in range(1, CONV_W):
        y = y + _shift_rows(x, prev8, j) * w_ref[CONV_W - 1 - j:CONV_W - j, :]
    return y


def _shift_matrices(t):
    r = np.arange(t)[:, None]
    c = np.arange(t)[None, :]
    return jnp.asarray(np.stack([(r - c == j) for j in range(1, CONV_W)]).astype(np.float32)).astype(BF16)


def _split2(x):
    hi = x.astype(BF16)
    lo = (x - hi.astype(F32)).astype(BF16)
    return jnp.concatenate([hi, lo], axis=1)


def _head_ones():
    r = lax.broadcasted_iota(jnp.int32, (2 * LANES, LANES), 0) % LANES
    c = lax.broadcasted_iota(jnp.int32, (2 * LANES, LANES), 1)
    return ((r < HEAD_DIM) == (c < HEAD_DIM)).astype(BF16)


def _pair_sum(x, ones2):
    return jnp.dot(_split2(x), ones2, preferred_element_type=F32)


def _pair_sum_lanes(x, m_e):
    se = jnp.sum(jnp.where(m_e, x, 0.0), axis=-1, keepdims=True)
    so = jnp.sum(jnp.where(m_e, 0.0, x), axis=-1, keepdims=True)
    return jnp.where(m_e, se, so)


def _pair_layer_norm(x, ones2):
    mu = _pair_sum(x, ones2) * (1.0 / HEAD_DIM)
    xc = x - mu
    var = _pair_sum(xc * xc, ones2) * (1.0 / HEAD_DIM)
    return xc * lax.rsqrt(var + GN_EPS)


def _block_diag_mask():
    r = lax.broadcasted_iota(jnp.int32, (LANES, LANES), 0)
    c = lax.broadcasted_iota(jnp.int32, (LANES, LANES), 1)
    return (r < HEAD_DIM) == (c < HEAD_DIM)


def _proj_kernel(x_ref, g_ref, w_ref, *o_refs):
    x = x_ref[...]
    rs = lax.rsqrt(jnp.mean(x * x, axis=-1, keepdims=True) + NORM_EPS)
    acc = jnp.dot((x * g_ref[...]).astype(BF16), w_ref[...], preferred_element_type=F32) * rs
    off = 0
    for o_ref in o_refs:
        n = o_ref.shape[1]
        o_ref[...] = acc[:, off:off + n].astype(o_ref.dtype)
        off += n


def _proj(x2, gain, w, outs, name):
    m, d = x2.shape
    n = w.shape[1]
    tm = TM_PROJ
    res = pl.pallas_call(
        _proj_kernel, grid=(m // tm,),
        in_specs=[pl.BlockSpec((tm, d), lambda i: (i, 0)),
                  pl.BlockSpec((1, d), lambda i: (0, 0)),
                  pl.BlockSpec((d, n), lambda i: (0, 0))],
        out_specs=[pl.BlockSpec((tm, wd), lambda i: (i, 0)) for wd, _ in outs],
        out_shape=[jax.ShapeDtypeStruct((m, wd), dt) for wd, dt in outs],
        compiler_params=pltpu.CompilerParams(dimension_semantics=("parallel",),
                                             vmem_limit_bytes=VMEM_LIMIT),
        name=name,
    )(x2, gain, w)
    return res


def _out_kernel(ya_ref, yb_ref, yc_ref, yd_ref, w_ref, g_ref, x_ref, o_ref):
    acc = jnp.dot(ya_ref[...].astype(BF16), w_ref[0], preferred_element_type=F32)
    acc += jnp.dot(yb_ref[...].astype(BF16), w_ref[1], preferred_element_type=F32)
    acc += jnp.dot(yc_ref[...].astype(BF16), w_ref[2], preferred_element_type=F32)
    acc += jnp.dot(yd_ref[...].astype(BF16), w_ref[3], preferred_element_type=F32)
    ms = jnp.mean(acc * acc, axis=-1, keepdims=True)
    o_ref[...] = x_ref[...] + (acc * lax.rsqrt(ms + NORM_EPS)) * g_ref[...]


def _out_proj(ys, w4, gain, x2):
    m, d = x2.shape
    tm = TM_PROJ
    yspec = pl.BlockSpec((tm, G), lambda i: (i, 0))
    return pl.pallas_call(
        _out_kernel, grid=(m // tm,),
        in_specs=[yspec, yspec, yspec, yspec,
                  pl.BlockSpec((4, G, d), lambda i: (0, 0, 0)),
                  pl.BlockSpec((1, d), lambda i: (0, 0)),
                  pl.BlockSpec((tm, d), lambda i: (i, 0))],
        out_specs=pl.BlockSpec((tm, d), lambda i: (i, 0)),
        out_shape=jax.ShapeDtypeStruct((m, d), F32),
        compiler_params=pltpu.CompilerParams(dimension_semantics=("parallel",),
                                             vmem_limit_bytes=VMEM_LIMIT),
        name="out_proj",
    )(*ys, w4, gain, x2)


def _mlstm_kernel(p_ref, g_ref, sh_ref, cw_ref, cb_ref, bcol_ref, nw_ref, o_ref,
                  prev_ref, c_ref, n_ref, m_ref):
    t = p_ref.shape[1]

    @pl.when(pl.program_id(1) == 0)
    def _():
        prev_ref[...] = jnp.zeros_like(prev_ref)
        c_ref[...] = jnp.zeros_like(c_ref)
        n_ref[...] = jnp.zeros_like(n_ref)
        m_ref[...] = jnp.zeros_like(m_ref)

    sb = sh_ref.shape[1]
    nsb = t // sb
    rws = [slice(i * sb, (i + 1) * sb) for i in range(nsb)]
    tail8 = lambda i: p_ref[0, (i + 1) * sb - 2 * SUBLANES:(i + 1) * sb, 0:2 * G].astype(F32)[SUBLANES:]
    qk = [_silu(_causal_conv(p_ref[0, rws[i], 0:2 * G], prev_ref[...] if i == 0 else tail8(i - 1),
                             sh_ref, cw_ref, cb_ref)) for i in range(nsb)]
    prev_ref[...] = tail8(nsb - 1)

    lane = lax.broadcasted_iota(jnp.int32, (sb, LANES), 1)
    m_e = lane < HEAD_DIM
    ones2 = _head_ones()
    ones_row = jnp.ones((2 * sb, LANES), BF16)

    gcol = [g_ref[0, rw, :] + bcol_ref[...] for rw in rws]
    cumc = [_cumsum(_log_sigmoid(gc), 0) for gc in gcol]
    grow = [jnp.where(lane < N_HEADS, gcol[i], cumc[i]).T for i in range(nsb)]

    m_e1 = lax.broadcasted_iota(jnp.int32, (1, LANES), 1) < HEAD_DIM
    tril = (lax.broadcasted_iota(jnp.int32, (sb, sb), 0) >= lax.broadcasted_iota(jnp.int32, (sb, sb), 1))
    bd = _block_diag_mask()

    pairs = range(N_PAIRS)
    sls = [slice(p * LANES, (p + 1) * LANES) for p in pairs]
    cp = [(i, p) for i in range(nsb) for p in pairs]
    ch = [(i, h) for i in range(nsb) for h in range(N_HEADS)]
    ip = lambda i, p: i * N_PAIRS + p
    ih = lambda i, h: i * N_HEADS + h
    sel = lambda xs, i, p: jnp.where(m_e, xs[ih(i, 2 * p)], xs[ih(i, 2 * p + 1)])
    sel1 = lambda xs, i, p: jnp.where(m_e1, xs[ih(i, 2 * p)], xs[ih(i, 2 * p + 1)])
    col = lambda base, p: slice(base + p * LANES, base + (p + 1) * LANES)
    nch = range(len(ch))

    q = [qk[i][:, sls[p]] for i, p in cp]
    k = [qk[i][:, col(G, p)] * (HEAD_DIM ** -0.5) for i, p in cp]
    vb = [p_ref[0, rws[i], col(2 * G, p)] for i, p in cp]
    kb = [x.astype(BF16) for x in k]
    qm = [(jnp.where(m_e, q[ip(i, h // 2)], 0.0) if h % 2 == 0
           else jnp.where(m_e, 0.0, q[ip(i, h // 2)])).astype(BF16) for i, h in ch]
    s_raw = [_dot_nt(qm[c], kb[ip(i, h // 2)]) for c, (i, h) in enumerate(ch)]

    cum_c = [cumc[i][:, N_HEADS + h:N_HEADS + h + 1] for i, h in ch]
    i_c = [gcol[i][:, h:h + 1] for i, h in ch]
    tot = [cum_c[c][sb - 1:sb, :] for c in nch]
    log_w = [tot[c] - cum_c[c] + i_c[c] for c in nch]
    lw_max = [jnp.max(log_w[c], axis=0, keepdims=True) for c in nch]
    m_st = [m_ref[p] for p in pairs]
    m_prev, m_new = [], []
    for c, (i, h) in enumerate(ch):
        mp = (m_st[h // 2][:, (h % 2) * HEAD_DIM:(h % 2) * HEAD_DIM + 1] if i == 0
              else m_new[ih(i - 1, h)])
        m_prev.append(mp)
        m_new.append(jnp.maximum(tot[c] + mp, lw_max[c]))
    log_d = [jnp.where(tril, cum_c[c] - grow[i][N_HEADS + h:N_HEADS + h + 1, :] + grow[i][h:h + 1, :], NEG)
             for c, (i, h) in enumerate(ch)]
    inter = [cum_c[c] + m_prev[c] for c in nch]
    m_t = [jnp.maximum(inter[c], jnp.max(log_d[c], axis=-1, keepdims=True)) for c in nch]
    s = [s_raw[c] * jnp.exp(log_d[c] - m_t[c]) for c in nch]
    s2 = [_split2(s[c]) for c in nch]
    num_h = [jnp.dot(s2[c][:, 0:sb], vb[ip(i, h // 2)], preferred_element_type=F32)
             for c, (i, h) in enumerate(ch)]
    den_h = [jnp.dot(s2[c], ones_row, preferred_element_type=F32) for c in nch]
    s_int_h = [jnp.exp(inter[c] - m_t[c]) for c in nch]
    wj_h = [jnp.exp(log_w[c] - m_new[c]) for c in nch]
    sc_h = [jnp.exp(tot[c] + m_prev[c] - m_new[c]) for c in nch]

    kw = [k[ip(i, p)] * sel(wj_h, i, p) for i, p in cp]
    vk = [_dot_tn(vb[x], kw[x]) for x in range(len(cp))]
    kw_sum = [jnp.sum(kw[x], axis=0, keepdims=True) for x in range(len(cp))]

    c_st = [c_ref[p] for p in pairs]
    n_st = [n_ref[p] for p in pairs]
    q_c, qn = [], []
    for x, (i, p) in enumerate(cp):
        q_c.append(_dot_nt(q[x], c_st[p]))
        qn.append(_pair_sum(q[x] * n_st[p], ones2))
        sc = sel1(sc_h, i, p)
        c_st[p] = sc * c_st[p] + jnp.where(bd, vk[x], 0.0)
        n_st[p] = sc * n_st[p] + kw_sum[x]
    for p in pairs:
        c_ref[p] = c_st[p]
        n_ref[p] = n_st[p]
        m_ref[p] = sel1(m_new, nsb - 1, p)

    for x, (i, p) in enumerate(cp):
        s_int = sel(s_int_h, i, p)
        num = sel(num_h, i, p) + s_int * q_c[x]
        den = sel(den_h, i, p) + s_int * qn[x]
        h_out = num / jnp.maximum(jnp.abs(den), jnp.exp(-sel(m_t, i, p)))
        o_gate = _sigmoid(p_ref[0, rws[i], col(3 * G, p)].astype(F32))
        z = p_ref[0, rws[i], col(4 * G, p)].astype(F32)
        y = _pair_layer_norm(h_out * o_gate, ones2) * nw_ref[:, sls[p]]
        o_ref[0, rws[i], sls[p]] = (y * _silu(z)).astype(o_ref.dtype)


def _mlstm(p_a, gates, conv_w, conv_b, i_bias, f_bias, norm_w):
    b, s, width = p_a.shape
    t = T_MLSTM
    nc = s // t
    bcol = jnp.zeros((1, LANES), F32).at[0, 0:N_HEADS].set(i_bias).at[0, N_HEADS:2 * N_HEADS].set(f_bias)
    const = lambda shape: pl.BlockSpec(shape, lambda bi, ci: (0,) * len(shape))
    return pl.pallas_call(
        _mlstm_kernel, grid=(b, nc),
        in_specs=[pl.BlockSpec((1, t, width), lambda bi, ci: (bi, ci, 0)),
                  pl.BlockSpec((1, t, LANES), lambda bi, ci: (bi, ci, 0)),
                  const((CONV_W - 1, MLSTM_CHUNK, MLSTM_CHUNK)),
                  const((CONV_W, 2 * G)), const((1, 2 * G)), const((1, LANES)), const((1, G))],
        out_specs=pl.BlockSpec((1, t, G), lambda bi, ci: (bi, ci, 0)),
        out_shape=jax.ShapeDtypeStruct((b, s, G), BF16),
        scratch_shapes=[pltpu.VMEM((SUBLANES, 2 * G), F32),
                        pltpu.VMEM((N_PAIRS, LANES, LANES), F32),
                        pltpu.VMEM((N_PAIRS, 1, LANES), F32),
                        pltpu.VMEM((N_PAIRS, 1, LANES), F32)],
        compiler_params=pltpu.CompilerParams(dimension_semantics=("parallel", "arbitrary"),
                                             vmem_limit_bytes=VMEM_LIMIT),
        name="mlstm",
    )(p_a, gates, _shift_matrices(MLSTM_CHUNK), conv_w, conv_b.reshape(1, -1), bcol, norm_w.reshape(1, -1))


def _stack(x, m_e):
    return jnp.concatenate([jnp.where(m_e, x, 0.0), jnp.where(m_e, 0.0, x)], axis=0)


def _rwkv_kernel(p_ref, mu_ref, wlo_ref, b0_ref, kk_ref, ka_ref, rk_ref, gw_ref, gb_ref,
                 msk_ref, o_ref, prev_ref, s_ref):
    t = p_ref.shape[1]
    sh_w = 3 * G + 2 * LORA

    @pl.when(pl.program_id(1) == 0)
    def _():
        prev_ref[...] = jnp.zeros_like(prev_ref)
        s_ref[...] = jnp.zeros_like(s_ref)

    x = p_ref[0, :, 0:sh_w].astype(F32)
    xs = x + mu_ref[...] * (_shift_rows(x, prev_ref[...], 1) - x)
    prev_ref[...] = x[t - SUBLANES:t, :]

    r = xs[:, 0:G]
    k = xs[:, G:2 * G]
    v = xs[:, 2 * G:3 * G]
    lo = xs[:, 3 * G:sh_w]
    lane = lax.broadcasted_iota(jnp.int32, (t, LANES), 1)
    m_e = lane < HEAD_DIM
    lo = jnp.where(m_e, jnp.tanh(lo), lo)
    pre = _dot(lo, wlo_ref[...]) + b0_ref[...]
    log_w = -W_DECAY_SCALE * _sigmoid(pre[:, 0:G])
    a = _sigmoid(pre[:, G:2 * G])

    kappa = k * kk_ref[...]
    k_til = k * (1.0 + (a - 1.0) * ka_ref[...])
    rkk = r * rk_ref[...] * k_til
    kap2 = kappa * kappa

    strict = msk_ref[0] > 0.5
    incl = msk_ref[1] > 0.5
    eye = msk_ref[2]

    ch = RWKV_CHUNK
    n2 = 2 * ch
    m_c = lax.broadcasted_iota(jnp.int32, (ch, LANES), 1) < HEAD_DIM
    pairs = range(N_PAIRS)
    chunks = range(t // ch)
    chains = [(j, p) for j in chunks for p in pairs]
    rows = [slice(j * ch, (j + 1) * ch) for j in chunks]
    sls = [slice(p * LANES, (p + 1) * LANES) for p in pairs]
    blk = lambda arr, c: arr[rows[c[0]], sls[c[1]]]

    log_p = [_cumsum(log_w[rw], 0) for rw in rows]
    p_in = [jnp.exp(lp) for lp in log_p]
    p_inv = [jnp.exp(-lp) for lp in log_p]
    p_prev = [jnp.exp(lp - log_w[rw]) for lp, rw in zip(log_p, rows)]
    p_end = [jnp.exp(lp[ch - 1:ch] - lp) for lp in log_p]
    p_l = [jnp.exp(lp[ch - 1:ch]) for lp in log_p]
    dec = lambda arr, c: arr[c[0]][:, sls[c[1]]]

    kh = [blk(kappa, c) * lax.rsqrt(_pair_sum_lanes(blk(kap2, c), m_c) + 1e-12) for c in chains]
    b_v = [blk(a, c) * kh[i] for i, c in enumerate(chains)]
    stk = lambda val: _stack(val, m_c).astype(BF16)
    khs = [stk(kh[i] * dec(p_prev, c)) for i, c in enumerate(chains)]
    rs = [stk(blk(r, c) * dec(p_in, c)) for c in chains]
    bs = [stk(b_v[i] * dec(p_inv, c)) for i, c in enumerate(chains)]
    kts = [stk(blk(k_til, c) * dec(p_inv, c)) for c in chains]
    vs = [stk(blk(v, c)) for c in chains]
    kte = [stk(blk(k_til, c) * dec(p_end, c)) for c in chains]
    be = [stk(b_v[i] * dec(p_end, c)) for i, c in enumerate(chains)]
    idx = range(len(chains))

    a_rb, av, wu = [], [], []
    for g0 in range(0, len(chains), RWKV_GROUP):
        ids = range(g0, min(g0 + RWKV_GROUP, len(chains)))
        g = {i: _dot_nt(jnp.concatenate([khs[i], rs[i]], axis=0),
                        jnp.concatenate([bs[i], kts[i]], axis=0)) for i in ids}
        a_ub = {i: jnp.where(strict, g[i][0:n2, 0:n2], 0.0) for i in ids}
        a_k = {i: jnp.concatenate([jnp.where(strict, g[i][0:n2, n2:], 0.0),
                                   jnp.where(incl, g[i][n2:, n2:], 0.0)], axis=0).astype(BF16) for i in ids}
        a_rb += [jnp.where(incl, g[i][n2:, 0:n2], 0.0).astype(BF16) for i in ids]
        av_g = {i: _dot(a_k[i], vs[i]) for i in ids}
        av += [av_g[i] for i in ids]

        xinv = {i: eye - a_ub[i] * msk_ref[3] for i in ids}
        for lvl in range(4, 9):
            xa = {i: _dot(xinv[i], a_ub[i] * msk_ref[lvl]) for i in ids}
            xinv = {i: xinv[i] - _dot(xa[i], xinv[i]) for i in ids}
        wu += [_dot(xinv[i], jnp.concatenate([khs[i], av_g[i][0:n2].astype(BF16)], axis=1)) for i in ids]

    bonus_sum = [_pair_sum_lanes(blk(rkk, c), m_c) for c in chains]
    s_st = [s_ref[p] for p in pairs]
    ys = []
    for j in chunks:
        ids = [j * N_PAIRS + p for p in pairs]
        ws = [_dot_nt(jnp.concatenate([wu[i][:, 0:LANES].astype(BF16), rs[i]], axis=0), s_st[p])
              for p, i in enumerate(ids)]
        ub = [(-(ws[p][0:n2] + wu[i][:, LANES:])).astype(BF16) for p, i in enumerate(ids)]
        ys += [ws[p][n2:] + av[i][n2:] + _dot(a_rb[i], ub[p]) for p, i in enumerate(ids)]
        s_st = [s_st[p] * p_l[j][:, sls[p]] + _dot_tn(jnp.concatenate([vs[i], ub[p]], axis=0),
                                                      jnp.concatenate([kte[i], be[i]], axis=0))
                for p, i in enumerate(ids)]
    for p in pairs:
        s_ref[p] = s_st[p]

    wkv = [ys[i][0:ch] + ys[i][ch:n2] for i in idx]
    mu = [_pair_sum_lanes(wkv[i], m_c) * (1.0 / HEAD_DIM) for i in idx]
    xc = [wkv[i] - mu[i] for i in idx]
    var = [_pair_sum_lanes(xc[i] * xc[i], m_c) * (1.0 / HEAD_DIM) for i in idx]
    for i, (j, p) in enumerate(chains):
        z = p_ref[0, rows[j], sh_w + p * LANES:sh_w + (p + 1) * LANES].astype(F32)
        y = (xc[i] * lax.rsqrt(var[i] + GN_EPS)) * gw_ref[:, sls[p]] + gb_ref[:, sls[p]] \
            + bonus_sum[i] * blk(v, chains[i])
        o_ref[0, rows[j], sls[p]] = (y * _silu(z)).astype(o_ref.dtype)


def _rwkv_masks():
    n = 2 * RWKV_CHUNK
    r = np.arange(n)[:, None]
    c = np.arange(n)[None, :]
    ms = [r > c, r >= c, r == c]
    b = 1
    while b < RWKV_CHUNK:
        ms.append((r // (2 * b) == c // (2 * b)) & (r % (2 * b) >= b) & (c % (2 * b) < b))
        b *= 2
    return jnp.asarray(np.stack(ms).astype(np.float32))


def _rwkv(p_b, mu, w_up, w0, a_up, a0, k_k, k_a, r_k, gn_w, gn_b):
    b, s, width = p_b.shape
    t = T_RWKV
    sh_w = 3 * G + 2 * LORA
    w_lo = jnp.zeros((LANES, 2 * G), F32).at[0:LORA, 0:G].set(w_up).at[LORA:, G:].set(a_up).astype(BF16)
    b0 = jnp.concatenate([w0, a0]).reshape(1, 2 * G)
    row = lambda a: a.reshape(1, -1)
    const = lambda shape: pl.BlockSpec(shape, lambda bi, ci: (0,) * len(shape))
    return pl.pallas_call(
        _rwkv_kernel, grid=(b, s // t),
        in_specs=[pl.BlockSpec((1, t, width), lambda bi, ci: (bi, ci, 0)),
                  const((1, sh_w)), const((LANES, 2 * G)), const((1, 2 * G)),
                  const((1, G)), const((1, G)), const((1, G)), const((1, G)), const((1, G)),
                  const((9, 2 * RWKV_CHUNK, 2 * RWKV_CHUNK))],
        out_specs=pl.BlockSpec((1, t, G), lambda bi, ci: (bi, ci, 0)),
        out_shape=jax.ShapeDtypeStruct((b, s, G), BF16),
        scratch_shapes=[pltpu.VMEM((SUBLANES, sh_w), F32),
                        pltpu.VMEM((N_PAIRS, LANES, LANES), F32)],
        compiler_params=pltpu.CompilerParams(dimension_semantics=("parallel", "arbitrary"),
                                             vmem_limit_bytes=VMEM_LIMIT),
        name="rwkv7",
    )(p_b, row(mu), w_lo, b0, row(k_k), row(k_a), row(r_k), row(gn_w), row(gn_b), _rwkv_masks())


def _lru_kernel(p_ref, cw_ref, cb_ref, wr_ref, br_ref, wi_ref, bi_ref, lam_ref, o_ref,
                prev_ref, h_ref):
    t = p_ref.shape[1]

    @pl.when(pl.program_id(1) == 0)
    def _():
        prev_ref[...] = jnp.zeros_like(prev_ref)
        h_ref[...] = jnp.zeros_like(h_ref)

    x = p_ref[0, :, 0:G].astype(F32)
    xc = _causal_conv_rolls(x, prev_ref[...], cw_ref, cb_ref)
    prev_ref[...] = x[t - SUBLANES:t, :]

    xb = xc.astype(BF16)
    r = _sigmoid(jnp.dot(xb, wr_ref[...], preferred_element_type=F32) + br_ref[...])
    i = _sigmoid(jnp.dot(xb, wi_ref[...], preferred_element_type=F32) + bi_ref[...])
    log_a = (-LRU_C * r) * _softplus(-lam_ref[...])
    a = jnp.exp(log_a)
    th = jnp.tanh(log_a)
    u = jnp.sqrt(-2.0 * th / (1.0 - th)) * (i * xc)

    sub = lax.broadcasted_iota(jnp.int32, (SUBLANES, G), 0)
    carry = h_ref[...]
    groups = []
    for i in range(t // SUBLANES):
        rows = slice(i * SUBLANES, (i + 1) * SUBLANES)
        ug, ag = u[rows], a[rows]
        d = 1
        while d < SUBLANES:
            keep = sub >= d
            ug = ug + ag * jnp.where(keep, pltpu.roll(ug, d, axis=0), 0.0)
            ag = ag * jnp.where(keep, pltpu.roll(ag, d, axis=0), 1.0)
            d *= 2
        hg = ug + ag * carry
        carry = hg[SUBLANES - 1:SUBLANES]
        groups.append(hg)
    h_ref[...] = carry
    h = jnp.concatenate(groups, axis=0)
    o_ref[0] = (h * _silu(p_ref[0, :, G:2 * G].astype(F32))).astype(o_ref.dtype)


def _block_diag_weight(w):
    eye = jnp.eye(N_HEADS, dtype=w.dtype)
    return jnp.einsum("hij,hg->higj", w, eye).reshape(G, G)


def _lru(p_c, conv_w, conv_b, w_r, b_r, w_i, b_i, lam):
    b, s, width = p_c.shape
    t = T_LRU
    row = lambda a: a.reshape(1, -1)
    const = lambda shape: pl.BlockSpec(shape, lambda bi, ci: (0,) * len(shape))
    return pl.pallas_call(
        _lru_kernel, grid=(b, s // t),
        in_specs=[pl.BlockSpec((1, t, width), lambda bi, ci: (bi, ci, 0)),
                  const((CONV_W, G)), const((1, G)), const((G, G)), const((1, G)),
                  const((G, G)), const((1, G)), const((1, G))],
        out_specs=pl.BlockSpec((1, t, G), lambda bi, ci: (bi, ci, 0)),
        out_shape=jax.ShapeDtypeStruct((b, s, G), BF16),
        scratch_shapes=[pltpu.VMEM((SUBLANES, G), F32), pltpu.VMEM((1, G), F32)],
        compiler_params=pltpu.CompilerParams(dimension_semantics=("parallel", "arbitrary"),
                                             vmem_limit_bytes=VMEM_LIMIT),
        name="rglru",
    )(p_c, conv_w, row(conv_b), _block_diag_weight(w_r).astype(BF16), row(b_r),
      _block_diag_weight(w_i).astype(BF16), row(b_i), row(lam))


def _ret_kernel(p_ref, cos_ref, sin_ref, dm_ref, xi_ref, zeta_ref, g_ref, nw_ref, o_ref, r_ref):
    t = p_ref.shape[1]

    @pl.when(pl.program_id(1) == 0)
    def _():
        r_ref[...] = jnp.zeros_like(r_ref)

    sb = dm_ref.shape[1]
    lane = lax.broadcasted_iota(jnp.int32, (sb, LANES), 1)
    m_e = lane < HEAD_DIM
    first_half = (lane % HEAD_DIM) < (HEAD_DIM // 2)
    bd = _block_diag_mask()
    ones2 = _head_ones()

    def rope(x, rw):
        swapped = jnp.where(first_half, pltpu.roll(x, LANES - HEAD_DIM // 2, axis=1),
                            pltpu.roll(x, HEAD_DIM // 2, axis=1))
        return x * cos_ref[rw, :] + swapped * sin_ref[rw, :]

    pairs = range(N_PAIRS)
    sls = [slice(p * LANES, (p + 1) * LANES) for p in pairs]
    rws = [slice(i * sb, (i + 1) * sb) for i in range(t // sb)]
    cp = [(rw, p) for rw in rws for p in pairs]
    ch = [(i, half) for i in range(len(cp)) for half in range(2)]
    col = lambda base, p: slice(base + p * LANES, base + (p + 1) * LANES)
    q = [rope(p_ref[0, rw, col(0, p)].astype(F32), rw) for rw, p in cp]
    k = [rope(p_ref[0, rw, col(G, p)].astype(F32), rw) * (HEAD_DIM ** -0.5) for rw, p in cp]
    vb = [p_ref[0, rw, col(2 * G, p)] for rw, p in cp]
    kb = [x.astype(BF16) for x in k]
    qm = [(jnp.where(m_e, q[i], 0.0) if half == 0 else jnp.where(m_e, 0.0, q[i])).astype(BF16)
          for i, half in ch]
    s = [_dot_nt(qm[c], kb[i]) * dm_ref[2 * cp[i][1] + half] for c, (i, half) in enumerate(ch)]
    o_h = [_dot(s[c], vb[i]) for c, (i, _) in enumerate(ch)]
    kv = [_dot_tn(k[i] * zeta_ref[p], vb[i]) for i, (_, p) in enumerate(cp)]
    r_st = [r_ref[p] for p in pairs]
    q_r = []
    for i, (_, p) in enumerate(cp):
        q_r.append(_dot(q[i], r_st[p]))
        r_st[p] = g_ref[p] * r_st[p] + jnp.where(bd, kv[i], 0.0)
    for p in pairs:
        r_ref[p] = r_st[p]
    for i, (rw, p) in enumerate(cp):
        o = jnp.where(m_e, o_h[2 * i], o_h[2 * i + 1]) + xi_ref[p] * q_r[i]
        z = p_ref[0, rw, col(3 * G, p)].astype(F32)
        o_ref[0, rw, sls[p]] = (_pair_layer_norm(o, ones2) * nw_ref[:, sls[p]] * _silu(z)).astype(o_ref.dtype)


def _ret_tables(s):
    t = RET_SB
    half = HEAD_DIM // 2
    pos = jnp.arange(s, dtype=F32)
    inv_freq = ROPE_THETA ** (-jnp.arange(half, dtype=F32) / half)
    ang = pos[:, None] * inv_freq[None, :]
    cos = jnp.tile(jnp.cos(ang), (1, LANES // half))
    sin = jnp.sin(ang)
    sin = jnp.tile(jnp.concatenate([-sin, sin], axis=-1), (1, LANES // HEAD_DIM))
    log_g = jnp.log1p(-jnp.exp2(-5.0 - jnp.arange(N_HEADS, dtype=F32)))
    idx = jnp.arange(t, dtype=F32)
    chunk = jnp.arange(t) // RET_CHUNK
    visible = chunk[:, None] >= chunk[None, :]
    dm = jnp.where(visible[None], jnp.exp(log_g[:, None, None] * jnp.abs(idx[:, None] - idx[None, :])), 0.0)
    pair_lanes = lambda a: jnp.repeat(a.reshape(N_PAIRS, 2, -1), HEAD_DIM, axis=1).transpose(0, 2, 1)
    xi = pair_lanes(jnp.exp(log_g[:, None] * (idx + 1.0)))
    zeta = pair_lanes(jnp.exp(log_g[:, None] * (t - 1.0 - idx)))
    g_blk = pair_lanes(jnp.exp(log_g * t)[:, None])
    return cos, sin, dm, xi, zeta, g_blk


def _retention(p_d, norm_w, tables):
    b, s, width = p_d.shape
    t = T_RET
    cos, sin, dm, xi, zeta, g_blk = tables
    const = lambda shape: pl.BlockSpec(shape, lambda bi, ci: (0,) * len(shape))
    return pl.pallas_call(
        _ret_kernel, grid=(b, s // t),
        in_specs=[pl.BlockSpec((1, t, width), lambda bi, ci: (bi, ci, 0)),
                  pl.BlockSpec((t, LANES), lambda bi, ci: (ci, 0)),
                  pl.BlockSpec((t, LANES), lambda bi, ci: (ci, 0)),
                  const((N_HEADS, RET_SB, RET_SB)), const((N_PAIRS, RET_SB, LANES)),
                  const((N_PAIRS, RET_SB, LANES)), const((N_PAIRS, 1, LANES)), const((1, G))],
        out_specs=pl.BlockSpec((1, t, G), lambda bi, ci: (bi, ci, 0)),
        out_shape=jax.ShapeDtypeStruct((b, s, G), BF16),
        scratch_shapes=[pltpu.VMEM((N_PAIRS, LANES, LANES), F32)],
        compiler_params=pltpu.CompilerParams(dimension_semantics=("parallel", "arbitrary"),
                                             vmem_limit_bytes=VMEM_LIMIT),
        name="retention",
    )(p_d, cos, sin, dm, xi, zeta, g_blk, norm_w.reshape(1, -1))


def kernel(x, norm_pre, norm_post, w_in, w_out, mlstm_conv_w, mlstm_conv_b, mlstm_i_bias, mlstm_f_bias, mlstm_norm_w, rwkv_mu, rwkv_w_up, rwkv_w0, rwkv_a_up, rwkv_a0, rwkv_k_k, rwkv_k_a, rwkv_r_k, rwkv_gn_w, rwkv_gn_b, lru_conv_w, lru_conv_b, lru_w_r, lru_b_r, lru_w_i, lru_b_i, lru_lambda, ret_norm_w):
    b, s, d = x.shape
    depth = w_in.shape[0]
    tables = _ret_tables(s)
    x2 = x.reshape(b * s, d)
    for l in range(depth):
        w = w_in[l]
        gain = norm_pre[l].reshape(1, d)
        w_a = jnp.concatenate([w[:, _A0:_B0], jnp.zeros((d, LANES - 2 * N_HEADS), F32)],
                              axis=1).astype(BF16)
        p_a, gates = _proj(x2, gain, w_a, [(_AG, BF16), (LANES, F32)], "in_proj_mlstm")
        p_b, = _proj(x2, gain, w[:, _B0:_C0].astype(BF16), [(_C0 - _B0, BF16)], "in_proj_rwkv7")
        p_c, p_d = _proj(x2, gain, w[:, _C0:_END].astype(BF16),
                         [(_D0 - _C0, BF16), (_END - _D0, BF16)], "in_proj_rglru_retention")

        y_a = _mlstm(p_a.reshape(b, s, -1), gates.reshape(b, s, -1), mlstm_conv_w[l], mlstm_conv_b[l],
                     mlstm_i_bias[l], mlstm_f_bias[l], mlstm_norm_w[l])
        y_b = _rwkv(p_b.reshape(b, s, -1), rwkv_mu[l], rwkv_w_up[l], rwkv_w0[l], rwkv_a_up[l],
                    rwkv_a0[l], rwkv_k_k[l], rwkv_k_a[l], rwkv_r_k[l], rwkv_gn_w[l], rwkv_gn_b[l])
        y_c = _lru(p_c.reshape(b, s, -1), lru_conv_w[l], lru_conv_b[l], lru_w_r[l], lru_b_r[l],
                   lru_w_i[l], lru_b_i[l], lru_lambda[l])
        y_d = _retention(p_d.reshape(b, s, -1), ret_norm_w[l], tables)

        ys = [y.reshape(b * s, G) for y in (y_a, y_b, y_c, y_d)]
        x2 = _out_proj(ys, w_out[l].reshape(4, G, d).astype(BF16), norm_post[l].reshape(1, d), x2)
    return x2.reshape(b, s, d)
```
